```python
import math
import jax, jax.numpy as jnp
from jax import lax
import numpy as np

D_MODEL = 1024
BATCH = 8
SEQ = 4096
DEPTH = 1

DIFF_HEADS = 8
DIFF_HD = 64
NSA_HEADS = 16
NSA_KV = 4
NSA_HD = 64
CMP_LEN = 32
CMP_STRIDE = 16
SLC_LEN = 64
SLC_TOPN = 16
WIN = 512
PHI_HID = 256
Q_BLOCK = 128
NSA_Q_BLOCK = 64
N_EXPERTS = 32
TOP_K = 4
D_FF = 1024
SWIGLU_LIMIT = 7.0
SWIGLU_ALPHA = 1.702
MOE_BLOCK = 512
PLE_DIM = 256
LN_EPS = 1e-5
NEG = -1e30
BIG = 1e30
DEEPNORM_ALPHA = (2.0 * DEPTH) ** 0.25
DEEPNORM_BETA = (8.0 * DEPTH) ** -0.25
DIFF_QK = DIFF_HEADS * 2 * DIFF_HD
DIFF_V = DIFF_HEADS * 2 * DIFF_HD
NSA_Q = NSA_HEADS * NSA_HD
NSA_KVW = NSA_KV * NSA_HD
NSA_GATES = NSA_HEADS * 3
IN_SIZES = (DIFF_QK, DIFF_QK, DIFF_V, NSA_Q) + (NSA_KVW,) * 6 + (NSA_GATES,)
C_IN = sum(IN_SIZES)

kernel_name = 'hybrid_diffattn_nsa_moe_deepnorm'


def _alibi_slopes(n):
    return jnp.asarray(2.0 ** (-8.0 * np.arange(1, n + 1) / n), jnp.float32)


def _layer_norm(x, g, b):
    xf = x.astype(jnp.float32)
    mu = jnp.mean(xf, -1, keepdims=True)
    var = jnp.mean(jnp.square(xf - mu), -1, keepdims=True)
    return ((xf - mu) * lax.rsqrt(var + LN_EPS) * g + b).astype(x.dtype)


def _rms_norm(x, g):
    xf = x.astype(jnp.float32)
    return (xf * lax.rsqrt(jnp.mean(xf * xf, -1, keepdims=True) + LN_EPS) * g).astype(x.dtype)


def _masked_softmax(s, mask):
    pr = jax.nn.softmax(jnp.where(mask, s, NEG), axis=-1)
    return jnp.where(mask, pr, 0.0)


def _diff_attention(q, k, v, lq1, lk1, lq2, lk2, subln_g, lambda_init):
    B, S = q.shape[:2]
    lam = (jnp.exp(jnp.sum(lq1.astype(jnp.float32) * lk1))
           - jnp.exp(jnp.sum(lq2.astype(jnp.float32) * lk2)) + lambda_init)
    slopes = _alibi_slopes(DIFF_HEADS)[None, :, None, None, None]
    scale = DIFF_HD ** -0.5
    outs = []
    for qs in range(0, S, Q_BLOCK):
        qe = qs + Q_BLOCK
        s = jnp.einsum('bqhcd,bkhcd->bhcqk', q[:, qs:qe], k[:, :qe]).astype(jnp.float32) * scale
        dist = jnp.arange(qs, qe)[:, None] - jnp.arange(qe)[None, :]
        s = s - slopes * dist.astype(jnp.float32)
        pr = _masked_softmax(s, dist >= 0)
        a = pr[:, :, 0] - lam * pr[:, :, 1]
        outs.append(jnp.einsum('bhqk,bkhe->bqhe', a.astype(v.dtype), v[:, :qe]))
    o = _rms_norm(jnp.concatenate(outs, axis=1), subln_g) * (1.0 - lambda_init)
    return o.reshape(B, S, DIFF_V).astype(q.dtype)


def _nsa_attention(q, kc, vc, ks, vs, kw, vw, gates, pos_k, pos_v, phi_k1, phi_k2, phi_v1, phi_v2):
    B, S = q.shape[:2]
    G, HPG, QB = NSA_KV, NSA_HEADS // NSA_KV, NSA_Q_BLOCK
    scale = NSA_HD ** -0.5
    slopes = _alibi_slopes(NSA_HEADS).reshape(G, HPG)
    n_cmp = (S - CMP_LEN) // CMP_STRIDE + 1
    cmp_idx = np.arange(n_cmp)[:, None] * CMP_STRIDE + np.arange(CMP_LEN)[None, :]
    cmp_pos = jnp.asarray(cmp_idx[:, -1], jnp.int32)

    def compress(t, pos, w1, w2):
        blk = t[:, cmp_idx] + pos[:, None, :]
        blk = blk.transpose(0, 1, 3, 2, 4).reshape(B, n_cmp, G, CMP_LEN * NSA_HD)
        return jax.nn.silu(blk @ w1) @ w2

    k_cmp = compress(kc, pos_k, phi_k1, phi_k2)
    v_cmp = compress(vc, pos_v, phi_v1, phi_v2)
    n_slc = S // SLC_LEN
    n_top = min(SLC_TOPN, n_slc)
    c_lo = np.arange(n_cmp) * CMP_STRIDE
    s_lo = np.arange(n_slc) * SLC_LEN
    ov = (np.minimum(c_lo[:, None] + CMP_LEN, s_lo[None, :] + SLC_LEN)
          - np.maximum(c_lo[:, None], s_lo[None, :]))
    cmp_to_slc = jnp.asarray(np.clip(ov, 0, None) / CMP_LEN, jnp.float32)
    ks_blk = ks.reshape(B, n_slc, SLC_LEN, G, NSA_HD).transpose(0, 3, 1, 2, 4)
    vs_blk = vs.reshape(B, n_slc, SLC_LEN, G, NSA_HD).transpose(0, 3, 1, 2, 4)
    kw_pad = jnp.pad(kw, ((0, 0), (WIN, 0), (0, 0), (0, 0)))
    vw_pad = jnp.pad(vw, ((0, 0), (WIN, 0), (0, 0), (0, 0)))
    b_ix = jnp.arange(B)[:, None, None, None]
    g_ix = jnp.arange(G)[None, :, None, None]
    blk_ids = jnp.arange(n_slc, dtype=jnp.int32)
    nb = S // QB
    q_b = q.reshape(B, nb, QB, G, HPG, NSA_HD).transpose(1, 0, 2, 3, 4, 5)
    g_b = gates.reshape(B, nb, QB, G, HPG, 3).transpose(1, 0, 2, 3, 4, 5)
    starts = jnp.arange(nb, dtype=jnp.int32) * QB

    def block(args):
        qb, gb, qs = args
        t = qs + jnp.arange(QB, dtype=jnp.int32)
        d_c = t[:, None] - cmp_pos[None, :]
        sc = jnp.einsum('bqghd,bngd->bghqn', qb, k_cmp).astype(jnp.float32) * scale
        sc = sc - slopes[None, :, :, None, None] * d_c.astype(jnp.float32)
        pc = _masked_softmax(sc, d_c >= 0)
        o_cmp = jnp.einsum('bghqn,bngd->bqghd', pc.astype(v_cmp.dtype), v_cmp)
        imp = jnp.einsum('bghqn,nj->bgqj', pc, cmp_to_slc)
        cur = (t // SLC_LEN)[:, None]
        forced = (blk_ids[None] == 0) | (blk_ids[None] == cur) | (blk_ids[None] == cur - 1)
        imp = jnp.where(forced, BIG, jnp.where(blk_ids[None] <= cur, imp, NEG))
        _, sel = lax.top_k(imp, n_top)
        k_sel = ks_blk[b_ix, g_ix, sel]
        v_sel = vs_blk[b_ix, g_ix, sel]
        d_s = t[None, None, :, None, None] - (sel[..., None] * SLC_LEN
                                              + jnp.arange(SLC_LEN, dtype=jnp.int32))
        ss = jnp.einsum('bqghd,bgqnld->bghqnl', qb, k_sel).astype(jnp.float32) * scale
        ss = ss - slopes[None, :, :, None, None, None] * d_s[:, :, None].astype(jnp.float32)
        ps = _masked_softmax(ss.reshape(B, G, HPG, QB, n_top * SLC_LEN),
                             (d_s >= 0).reshape(B, G, 1, QB, n_top * SLC_LEN))
        o_slc = jnp.einsum('bghqnl,bgqnld->bqghd',
                           ps.reshape(B, G, HPG, QB, n_top, SLC_LEN).astype(v_sel.dtype), v_sel)
        k_win = lax.dynamic_slice_in_dim(kw_pad, qs, WIN + QB, axis=1)
        v_win = lax.dynamic_slice_in_dim(vw_pad, qs, WIN + QB, axis=1)
        d_w = t[:, None] - (qs - WIN + jnp.arange(WIN + QB, dtype=jnp.int32))[None, :]
        sw = jnp.einsum('bqghd,bkgd->bghqk', qb, k_win).astype(jnp.float32) * scale
        sw = sw - slopes[None, :, :, None, None] * d_w.astype(jnp.float32)
        pw = _masked_softmax(sw, (d_w >= 0) & (d_w < WIN) & (d_w <= t[:, None]))
        o_win = jnp.einsum('bghqk,bkgd->bqghd', pw.astype(v_win.dtype), v_win)
        out = gb[..., 0:1] * o_cmp + gb[..., 1:2] * o_slc + gb[..., 2:3] * o_win
        return out.astype(qb.dtype)

    o = lax.map(block, (q_b, g_b, starts))
    return o.transpose(1, 0, 2, 3, 4, 5).reshape(B, S, NSA_Q)


def _moe(h, w_router, b_router, w1, b1, w2, b2):
    B, S, D = h.shape
    xt = h.reshape(-1, D)
    T = xt.shape[0]
    logits = (xt @ w_router + b_router).astype(jnp.float32)
    top_v, top_e = lax.top_k(logits, TOP_K)
    gate = jax.nn.softmax(top_v, axis=-1)
    flat_e = top_e.reshape(-1).astype(jnp.int32)
    flat_g = gate.reshape(-1)
    flat_t = jnp.arange(T * TOP_K, dtype=jnp.int32) // TOP_K
    order = jnp.argsort(flat_e)
    se = flat_e[order]
    counts = jnp.zeros(N_EXPERTS, jnp.int32).at[flat_e].add(1)
    starts = jnp.cumsum(counts) - counts
    pcounts = (counts + MOE_BLOCK - 1) // MOE_BLOCK * MOE_BLOCK
    pend = jnp.cumsum(pcounts)
    pstart = pend - pcounts
    dest = pstart[se] + (jnp.arange(T * TOP_K, dtype=jnp.int32) - starts[se])
    n_blocks = (T * TOP_K + MOE_BLOCK - 1) // MOE_BLOCK + N_EXPERTS
    n_pad = n_blocks * MOE_BLOCK
    slot_tok = jnp.zeros(n_pad, jnp.int32).at[dest].set(flat_t[order])
    slot_gate = jnp.zeros(n_pad, jnp.float32).at[dest].set(flat_g[order])
    blk_expert = jnp.minimum(
        jnp.searchsorted(pend, jnp.arange(n_blocks, dtype=jnp.int32) * MOE_BLOCK, side='right'),
        N_EXPERTS - 1).astype(jnp.int32)

    def expert_block(args):
        e, tok = args
        hb = xt[tok] @ w1[e] + b1[e]
        x_glu, x_lin = jnp.split(hb, 2, axis=-1)
        x_glu = jnp.minimum(x_glu, SWIGLU_LIMIT)
        x_lin = jnp.clip(x_lin, -SWIGLU_LIMIT, SWIGLU_LIMIT)
        act = x_glu * jax.nn.sigmoid(SWIGLU_ALPHA * x_glu) * (x_lin + 1.0)
        return act @ w2[e] + b2[e]

    y = lax.map(expert_block, (blk_expert, slot_tok.reshape(n_blocks, MOE_BLOCK)))
    y = y.reshape(n_pad, D) * slot_gate[:, None].astype(y.dtype)
    out = jnp.zeros((T, D), y.dtype).at[slot_tok].add(y)
    return out.reshape(B, S, D).astype(h.dtype)


def setup_inputs(seed: int = 0) -> dict:
    key = jax.random.key(seed)
    keys = iter(jax.random.split(key, 40))

    def nrm(shape, scale):
        return jax.random.normal(next(keys), shape, jnp.float32) * scale

    L, D = DEPTH, D_MODEL
    return {
        'x': nrm((BATCH, SEQ, D), 1.0),
        'p': nrm((L, BATCH, SEQ, PLE_DIM), 1.0),
        'w_in': nrm((L, D, C_IN), D ** -0.5),
        'diff_lq1': nrm((L, DIFF_HD), 0.1),
        'diff_lk1': nrm((L, DIFF_HD), 0.1),
        'diff_lq2': nrm((L, DIFF_HD), 0.1),
        'diff_lk2': nrm((L, DIFF_HD), 0.1),
        'diff_subln_g': 1.0 + nrm((L, 2 * DIFF_HD), 0.02),
        'nsa_pos_k': nrm((L, CMP_LEN, NSA_HD), 0.1),
        'nsa_pos_v': nrm((L, CMP_LEN, NSA_HD), 0.1),
        'nsa_phi_k1': nrm((L, CMP_LEN * NSA_HD, PHI_HID), (CMP_LEN * NSA_HD) ** -0.5),
        'nsa_phi_k2': nrm((L, PHI_HID, NSA_HD), PHI_HID ** -0.5),
        'nsa_phi_v1': nrm((L, CMP_LEN * NSA_HD, PHI_HID), (CMP_LEN * NSA_HD) ** -0.5),
        'nsa_phi_v2': nrm((L, PHI_HID, NSA_HD), PHI_HID ** -0.5),
        'w_br_diff': nrm((L, DIFF_V, D), DIFF_V ** -0.5),
        'w_br_nsa': nrm((L, NSA_Q, D), NSA_Q ** -0.5),
        'w_mgate': nrm((L, D, 2 * D), D ** -0.5),
        'b_mgate': nrm((L, 2 * D), 0.01),
        'w_o': nrm((L, D, D), D ** -0.5 * DEEPNORM_BETA),
        'ln1_g': 1.0 + nrm((L, D), 0.02),
        'ln1_b': nrm((L, D), 0.01),
        'w_router': nrm((L, D, N_EXPERTS), D ** -0.5),
        'b_router': nrm((L, N_EXPERTS), 0.01),
        'w_e1': nrm((L, N_EXPERTS, D, 2 * D_FF), D ** -0.5),
        'b_e1': nrm((L, N_EXPERTS, 2 * D_FF), 0.01),
        'w_e2': nrm((L, N_EXPERTS, D_FF, D), D_FF ** -0.5 * DEEPNORM_BETA),
        'b_e2': nrm((L, N_EXPERTS, D), 0.01),
        'w_ple_gate': nrm((L, D, D), D ** -0.5),
        'b_ple_gate': nrm((L, D), 0.01),
        'w_ple_proj': nrm((L, PLE_DIM, D), PLE_DIM ** -0.5 * DEEPNORM_BETA),
        'ln2_g': 1.0 + nrm((L, D), 0.02),
        'ln2_b': nrm((L, D), 0.01),
    }


def reference(x, p, w_in, diff_lq1, diff_lk1, diff_lq2, diff_lk2, diff_subln_g,
              nsa_pos_k, nsa_pos_v, nsa_phi_k1, nsa_phi_k2, nsa_phi_v1, nsa_phi_v2,
              w_br_diff, w_br_nsa, w_mgate, b_mgate, w_o, ln1_g, ln1_b,
              w_router, b_router, w_e1, b_e1, w_e2, b_e2,
              w_ple_gate, b_ple_gate, w_ple_proj, ln2_g, ln2_b):
    B, S, D = x.shape
    splits = np.cumsum(IN_SIZES)[:-1].tolist()
    for i in range(DEPTH):
        lambda_init = 0.8 - 0.6 * math.exp(-0.3 * i)
        proj = x @ w_in[i]
        dq, dk, dv, nq, kc, vc, ks, vs, kw, vw, ng = jnp.split(proj, splits, axis=-1)
        o_diff = _diff_attention(
            dq.reshape(B, S, DIFF_HEADS, 2, DIFF_HD), dk.reshape(B, S, DIFF_HEADS, 2, DIFF_HD),
            dv.reshape(B, S, DIFF_HEADS, 2 * DIFF_HD),
            diff_lq1[i], diff_lk1[i], diff_lq2[i], diff_lk2[i], diff_subln_g[i], lambda_init)
        kvr = lambda t: t.reshape(B, S, NSA_KV, NSA_HD)
        o_nsa = _nsa_attention(
            nq.reshape(B, S, NSA_HEADS, NSA_HD), kvr(kc), kvr(vc), kvr(ks), kvr(vs), kvr(kw), kvr(vw),
            jax.nn.sigmoid(ng).reshape(B, S, NSA_HEADS, 3),
            nsa_pos_k[i], nsa_pos_v[i], nsa_phi_k1[i], nsa_phi_k2[i], nsa_phi_v1[i], nsa_phi_v2[i])
        mg = jax.nn.sigmoid(x @ w_mgate[i] + b_mgate[i])
        mixed = mg[..., :D] * (o_diff @ w_br_diff[i]) + mg[..., D:] * (o_nsa @ w_br_nsa[i])
        h = _layer_norm(DEEPNORM_ALPHA * x + mixed @ w_o[i], ln1_g[i], ln1_b[i])
        ple = jax.nn.sigmoid(h @ w_ple_gate[i] + b_ple_gate[i]) * (p[i] @ w_ple_proj[i])
        ffn = _moe(h, w_router[i], b_router[i], w_e1[i], b_e1[i], w_e2[i], b_e2[i])
        x = _layer_norm(DEEPNORM_ALPHA * h + ffn + ple, ln2_g[i], ln2_b[i])
    return x
```

```python
import functools
import math

import numpy as np
import jax
import jax.numpy as jnp
from jax import lax
from jax.experimental import pallas as pl
from jax.experimental.pallas import tpu as pltpu

D_MODEL = 1024
DIFF_HEADS = 8
DIFF_HD = 64
NSA_HEADS = 16
NSA_KV = 4
NSA_HD = 64
HPG = NSA_HEADS // NSA_KV
CMP_LEN = 32
CMP_STRIDE = 16
SLC_LEN = 64
SLC_SHIFT = 6
SLC_TOPN = 16
WIN = 512
PHI_HID = 256
N_EXPERTS = 32
TOP_K = 4
D_FF = 1024
SWIGLU_LIMIT = 7.0
SWIGLU_ALPHA = 1.702
MOE_BLOCK = 512
PLE_DIM = 256
LN_EPS = 1e-5
NEG = -1e30
BIG = 1e30
LOWEST = -3e38
LOG2E = 1.4426950408889634

DIFF_QK = DIFF_HEADS * 2 * DIFF_HD
DIFF_V = DIFF_HEADS * 2 * DIFF_HD
NSA_Q = NSA_HEADS * NSA_HD
NSA_KVW = NSA_KV * NSA_HD
NSA_GATES = NSA_HEADS * 3
IN_SIZES = (DIFF_QK, DIFF_QK, DIFF_V, NSA_Q) + (NSA_KVW,) * 6 + (NSA_GATES,)
C_MAIN = sum(IN_SIZES[:-1])

VMEM_LIMIT = 48 * 1024 * 1024

BF16 = jnp.bfloat16
F32 = jnp.float32


def _alibi_slopes_log2(n):
    return (2.0 ** (-8.0 * np.arange(1, n + 1) / n) * LOG2E).astype(np.float32)


def _cparams(sem, vmem=VMEM_LIMIT):
    return pltpu.CompilerParams(dimension_semantics=sem, vmem_limit_bytes=vmem)


def _mm_kernel(x_ref, w_ref, s_ref, b_ref, o_ref, xb_ref, *, act):
    @pl.when(pl.program_id(1) == 0)
    def _():
        xb_ref[...] = x_ref[...].astype(BF16)

    acc = jnp.dot(xb_ref[...], w_ref[...], preferred_element_type=F32)
    acc = acc * s_ref[...] + b_ref[...]
    if act == "sigmoid":
        acc = jax.nn.sigmoid(acc)
    o_ref[...] = acc.astype(o_ref.dtype)


def _matmul(x, w, scale, bias, *, act=None, out_dtype=F32, tm=512, tn=512):
    m, k = x.shape
    n = w.shape[1]
    tm = min(tm, m)
    tn = min(tn, n)
    assert m % tm == 0 and n % tn == 0
    return pl.pallas_call(
        functools.partial(_mm_kernel, act=act),
        grid=(m // tm, n // tn),
        in_specs=[
            pl.BlockSpec((tm, k), lambda i, j: (i, 0)),
            pl.BlockSpec((k, tn), lambda i, j: (0, j)),
            pl.BlockSpec((1, tn), lambda i, j: (0, j)),
            pl.BlockSpec((1, tn), lambda i, j: (0, j)),
        ],
        out_specs=pl.BlockSpec((tm, tn), lambda i, j: (i, j)),
        out_shape=jax.ShapeDtypeStruct((m, n), out_dtype),
        scratch_shapes=[pltpu.VMEM((tm, k), BF16)],
        compiler_params=_cparams(("parallel", "arbitrary")),
        name="dense_matmul",
    )(x, w, scale, bias)


def _online_step(s, v, m_ref, l_ref, acc_ref):
    m_old = m_ref[...]
    m_new = jnp.maximum(m_old, jnp.max(s, axis=-1, keepdims=True))
    alpha = jnp.exp2(m_old - m_new)
    p = jnp.exp2(s - m_new)
    l_ref[...] = alpha * l_ref[...] + jnp.sum(p, axis=-1, keepdims=True)
    acc_ref[...] = alpha * acc_ref[...] + jnp.dot(p.astype(BF16), v, preferred_element_type=F32)
    m_ref[...] = m_new


def _tri_steps(n):
    qi, ki = [], []
    for q in range(n):
        for k in range(q + 1):
            qi.append(q)
            ki.append(k)
    return np.asarray(qi, np.int32), np.asarray(ki, np.int32)


def _diff_kernel(qi_ref, ki_ref, sl_ref, q0_ref, q1_ref, k0_ref, k1_ref, v_ref,
                 lq1_ref, lk1_ref, lq2_ref, lk2_ref, g_ref, o_ref,
                 m0, l0, a0, m1, l1, a1, *, t, lambda_init):
    h = pl.program_id(1)
    step = pl.program_id(2)
    qt = qi_ref[step]
    kt = ki_ref[step]
    slope = sl_ref[h]

    @pl.when(kt == 0)
    def _():
        for m, l, a in ((m0, l0, a0), (m1, l1, a1)):
            m[...] = jnp.full(m.shape, NEG, F32)
            l[...] = jnp.zeros(l.shape, F32)
            a[...] = jnp.zeros(a.shape, F32)

    col = lax.broadcasted_iota(jnp.int32, (1, t), 1)
    bias = slope * ((kt - qt) * t + col).astype(F32)
    v = v_ref[0, 0]

    def scores(q_ref, k_ref):
        s = lax.dot_general(q_ref[0, 0], k_ref[0, 0], (((1,), (1,)), ((), ())),
                            preferred_element_type=F32)
        return s + bias

    @pl.when(kt < qt)
    def _():
        _online_step(scores(q0_ref, k0_ref), v, m0, l0, a0)
        _online_step(scores(q1_ref, k1_ref), v, m1, l1, a1)

    @pl.when(kt == qt)
    def _():
        row = lax.broadcasted_iota(jnp.int32, (t, t), 0)
        colf = lax.broadcasted_iota(jnp.int32, (t, t), 1)
        causal = colf <= row
        _online_step(jnp.where(causal, scores(q0_ref, k0_ref), NEG), v, m0, l0, a0)
        _online_step(jnp.where(causal, scores(q1_ref, k1_ref), NEG), v, m1, l1, a1)
        lam = (jnp.exp(jnp.sum(lq1_ref[...] * lk1_ref[...], axis=-1, keepdims=True))
               - jnp.exp(jnp.sum(lq2_ref[...] * lk2_ref[...], axis=-1, keepdims=True))
               + lambda_init)
        o = a0[...] / l0[...] - lam * (a1[...] / l1[...])
        ms = jnp.mean(o * o, axis=-1, keepdims=True)
        o = o * lax.rsqrt(ms + LN_EPS) * g_ref[...] * (1.0 - lambda_init)
        o_ref[0] = o.astype(o_ref.dtype)


def _diff_attention(q, k, v, lq1, lk1, lq2, lk2, g, lambda_init, *, t=512):
    b, _, s, _ = q.shape
    t = min(t, s)
    n = s // t
    qi, ki = _tri_steps(n)
    slopes = jnp.asarray(_alibi_slopes_log2(DIFF_HEADS))
    vec = lambda a: a.reshape(1, -1).astype(F32)
    qspec = lambda c: pl.BlockSpec((1, 1, t, DIFF_HD), lambda b_, h, st, qi, ki, sl: (b_, 2 * h + c, qi[st], 0))
    kspec = lambda c: pl.BlockSpec((1, 1, t, DIFF_HD), lambda b_, h, st, qi, ki, sl: (b_, 2 * h + c, ki[st], 0))
    small = lambda w: pl.BlockSpec((1, w), lambda b_, h, st, qi, ki, sl: (0, 0))
    grid_spec = pltpu.PrefetchScalarGridSpec(
        num_scalar_prefetch=3,
        grid=(b, DIFF_HEADS, len(qi)),
        in_specs=[qspec(0), qspec(1), kspec(0), kspec(1),
                  pl.BlockSpec((1, 1, t, 2 * DIFF_HD), lambda b_, h, st, qi, ki, sl: (b_, h, ki[st], 0)),
                  small(DIFF_HD), small(DIFF_HD), small(DIFF_HD), small(DIFF_HD), small(2 * DIFF_HD)],
        out_specs=pl.BlockSpec((1, t, 2 * DIFF_HD), lambda b_, h, st, qi, ki, sl: (b_, qi[st], h)),
        scratch_shapes=[pltpu.VMEM((t, 1), F32), pltpu.VMEM((t, 1), F32), pltpu.VMEM((t, 2 * DIFF_HD), F32),
                        pltpu.VMEM((t, 1), F32), pltpu.VMEM((t, 1), F32), pltpu.VMEM((t, 2 * DIFF_HD), F32)],
    )
    return pl.pallas_call(
        functools.partial(_diff_kernel, t=t, lambda_init=lambda_init),
        grid_spec=grid_spec,
        out_shape=jax.ShapeDtypeStruct((b, s, DIFF_V), BF16),
        compiler_params=_cparams(("parallel", "parallel", "arbitrary")),
        name="diff_attention",
    )(jnp.asarray(qi), jnp.asarray(ki), slopes, q, q, k, k, v,
      vec(lq1), vec(lk1), vec(lq2), vec(lk2), vec(g))


def _compress_kernel(a_ref, w1_ref, w2_ref, pos_ref, o_ref):
    a = a_ref[0, 0]
    half = CMP_STRIDE * NSA_HD
    top = jnp.dot(a, w1_ref[:half, :], preferred_element_type=F32)
    bot = jnp.dot(a, w1_ref[half:, :], preferred_element_type=F32)
    pos = pos_ref[...]
    pos_hi = pos.astype(BF16)
    pos_lo = (pos - pos_hi.astype(F32)).astype(BF16)
    c = (jnp.dot(pos_hi, w1_ref[...], preferred_element_type=F32)
         + jnp.dot(pos_lo, w1_ref[...], preferred_element_type=F32))[0:1]
    n = a.shape[0]
    hid = top + pltpu.roll(bot, n - 1, axis=0) + c
    hid = hid * jax.nn.sigmoid(hid)
    o_ref[0, 0] = jnp.dot(hid.astype(BF16), w2_ref[...], preferred_element_type=F32).astype(o_ref.dtype)


def _compress(t_halves, pos, w1, w2):
    b, g, nh, _ = t_halves.shape
    pos8 = jnp.broadcast_to(pos.reshape(1, -1).astype(F32), (8, CMP_LEN * NSA_HD))
    return pl.pallas_call(
        _compress_kernel,
        grid=(b, g),
        in_specs=[pl.BlockSpec((1, 1, nh, CMP_STRIDE * NSA_HD), lambda i, j: (i, j, 0, 0)),
                  pl.BlockSpec((CMP_LEN * NSA_HD, PHI_HID), lambda i, j: (0, 0)),
                  pl.BlockSpec((PHI_HID, NSA_HD), lambda i, j: (0, 0)),
                  pl.BlockSpec((8, CMP_LEN * NSA_HD), lambda i, j: (0, 0))],
        out_specs=pl.BlockSpec((1, 1, nh, NSA_HD), lambda i, j: (i, j, 0, 0)),
        out_shape=jax.ShapeDtypeStruct((b, g, nh, NSA_HD), BF16),
        compiler_params=_cparams(("parallel", "parallel")),
        name="nsa_compress",
    )(t_halves, w1.astype(BF16), w2.astype(BF16), pos8)


def _cmp_select_kernel(sl_ref, q_ref, kc_ref, vc_ref, o_ref, sb_ref, *, tq, n_cmp_pad, n_slc, n_top):
    g = pl.program_id(1)
    qt = pl.program_id(2)
    t_col = qt * tq + lax.broadcasted_iota(jnp.int32, (tq, 1), 0)
    cmp_pos = lax.broadcasted_iota(jnp.int32, (1, n_cmp_pad), 1) * CMP_STRIDE + (CMP_LEN - 1)
    dist = t_col - cmp_pos
    valid = dist >= 0
    distf = dist.astype(F32)
    kc = kc_ref[0, 0]
    vc = vc_ref[0, 0]
    psum = jnp.zeros((tq, n_cmp_pad), F32)
    for hh in range(HPG):
        slope = sl_ref[g * HPG + hh]
        s = lax.dot_general(q_ref[0, hh], kc, (((1,), (1,)), ((), ())), preferred_element_type=F32)
        s = jnp.where(valid, s - slope * distf, NEG)
        m = jnp.max(s, axis=-1, keepdims=True)
        p = jnp.where(valid, jnp.exp2(s - m), 0.0)
        l = jnp.maximum(jnp.sum(p, axis=-1, keepdims=True), 1e-30)
        p = p / l
        o_ref[0, hh] = jnp.dot(p.astype(BF16), vc, preferred_element_type=F32)
        psum = psum + p
    c_lo = lax.broadcasted_iota(jnp.int32, (n_cmp_pad, 128), 0) * CMP_STRIDE
    s_lo = lax.broadcasted_iota(jnp.int32, (n_cmp_pad, 128), 1) * SLC_LEN
    ov = jnp.minimum(c_lo + CMP_LEN, s_lo + SLC_LEN) - jnp.maximum(c_lo, s_lo)
    cmap = (jnp.maximum(ov, 0).astype(F32) * (1.0 / CMP_LEN)).astype(BF16)
    ps_hi = psum.astype(BF16)
    ps_lo = (psum - ps_hi.astype(F32)).astype(BF16)
    imp = (jnp.dot(ps_hi, cmap, preferred_element_type=F32)
           + jnp.dot(ps_lo, cmap, preferred_element_type=F32))
    imp_t = jnp.transpose(imp)[:n_slc]
    blk = lax.broadcasted_iota(jnp.int32, (n_slc, tq), 0)
    t_row = qt * tq + lax.broadcasted_iota(jnp.int32, (n_slc, tq), 1)
    cur = lax.shift_right_logical(t_row, SLC_SHIFT)
    forced = (blk == 0) | (blk == cur) | (blk == cur - 1)
    x = jnp.where(forced, BIG, jnp.where(blk <= cur, imp_t, NEG))
    sel = jnp.zeros((n_slc, tq), F32)
    for _ in range(n_top):
        mx = jnp.max(x, axis=0, keepdims=True)
        idx = jnp.min(jnp.where(x == mx, blk, n_slc), axis=0, keepdims=True)
        hit = blk == idx
        sel = jnp.where(hit, 1.0, sel)
        x = jnp.where(hit, LOWEST, x)
    if n_slc < 128:
        sel = jnp.concatenate([sel, jnp.zeros((128 - n_slc, tq), F32)], axis=0)
    sel_q = jnp.transpose(sel)[:, :n_slc]
    sb_ref[0, 0] = jnp.where(sel_q > 0.5, 0.0, NEG).astype(sb_ref.dtype)


def _cmp_select(q, k_cmp, v_cmp, *, tq=512):
    b, _, s, _ = q.shape
    tq = min(tq, s)
    n_cmp_pad = k_cmp.shape[2]
    n_slc = s // SLC_LEN
    n_top = min(SLC_TOPN, n_slc)
    slopes = jnp.asarray(_alibi_slopes_log2(NSA_HEADS))
    grid_spec = pltpu.PrefetchScalarGridSpec(
        num_scalar_prefetch=1,
        grid=(b, NSA_KV, s // tq),
        in_specs=[pl.BlockSpec((1, HPG, tq, NSA_HD), lambda i, j, k, sl: (i, j, k, 0)),
                  pl.BlockSpec((1, 1, n_cmp_pad, NSA_HD), lambda i, j, k, sl: (i, j, 0, 0)),
                  pl.BlockSpec((1, 1, n_cmp_pad, NSA_HD), lambda i, j, k, sl: (i, j, 0, 0))],
        out_specs=[pl.BlockSpec((1, HPG, tq, NSA_HD), lambda i, j, k, sl: (i, j, k, 0)),
                   pl.BlockSpec((1, 1, tq, n_slc), lambda i, j, k, sl: (i, j, k, 0))],
    )
    return pl.pallas_call(
        functools.partial(_cmp_select_kernel, tq=tq, n_cmp_pad=n_cmp_pad, n_slc=n_slc, n_top=n_top),
        grid_spec=grid_spec,
        out_shape=[jax.ShapeDtypeStruct((b, NSA_HEADS, s, NSA_HD), F32),
                   jax.ShapeDtypeStruct((b, NSA_KV, s, n_slc), BF16)],
        compiler_params=_cparams(("parallel", "parallel", "parallel")),
        name="nsa_cmp_select",
    )(slopes, q, k_cmp, v_cmp)


def _slc_kernel(qi_ref, ki_ref, sl_ref, q_ref, sb_ref, k_ref, v_ref, o_ref, m_sc, l_sc, a_sc, *, t, n_slc):
    h = pl.program_id(1)
    step = pl.program_id(2)
    qt = qi_ref[step]
    kt = ki_ref[step]
    slope = sl_ref[h]

    @pl.when(kt == 0)
    def _():
        m_sc[...] = jnp.full(m_sc.shape, NEG, F32)
        l_sc[...] = jnp.zeros(l_sc.shape, F32)
        a_sc[...] = jnp.zeros(a_sc.shape, F32)

    col = lax.broadcasted_iota(jnp.int32, (1, t), 1)
    bias = slope * ((kt - qt) * t + col).astype(F32)
    blk_of_key = lax.shift_right_logical(kt * t + lax.broadcasted_iota(jnp.int32, (n_slc, t), 1), SLC_SHIFT)
    onehot = jnp.where(blk_of_key == lax.broadcasted_iota(jnp.int32, (n_slc, t), 0), 1.0, 0.0).astype(BF16)
    s = lax.dot_general(q_ref[0, 0], k_ref[0, 0], (((1,), (1,)), ((), ())), preferred_element_type=F32)
    s = s + jnp.dot(sb_ref[0, 0], onehot, preferred_element_type=F32) + bias
    v = v_ref[0, 0]

    @pl.when(kt < qt)
    def _():
        _online_step(s, v, m_sc, l_sc, a_sc)

    @pl.when(kt == qt)
    def _():
        row = lax.broadcasted_iota(jnp.int32, (t, t), 0)
        colf = lax.broadcasted_iota(jnp.int32, (t, t), 1)
        _online_step(jnp.where(colf <= row, s, NEG), v, m_sc, l_sc, a_sc)
        o_ref[0, 0] = a_sc[...] / l_sc[...]


def _slc_attention(q, selbias, ks, vs, *, t=512):
    b, _, s, _ = q.shape
    t = min(t, s)
    n_slc = selbias.shape[-1]
    qi, ki = _tri_steps(s // t)
    slopes = jnp.asarray(_alibi_slopes_log2(NSA_HEADS))
    grid_spec = pltpu.PrefetchScalarGridSpec(
        num_scalar_prefetch=3,
        grid=(b, NSA_HEADS, len(qi)),
        in_specs=[pl.BlockSpec((1, 1, t, NSA_HD), lambda i, h, st, qi, ki, sl: (i, h, qi[st], 0)),
                  pl.BlockSpec((1, 1, t, n_slc), lambda i, h, st, qi, ki, sl: (i, h // HPG, qi[st], 0)),
                  pl.BlockSpec((1, 1, t, NSA_HD), lambda i, h, st, qi, ki, sl: (i, h // HPG, ki[st], 0)),
                  pl.BlockSpec((1, 1, t, NSA_HD), lambda i, h, st, qi, ki, sl: (i, h // HPG, ki[st], 0))],
        out_specs=pl.BlockSpec((1, 1, t, NSA_HD), lambda i, h, st, qi, ki, sl: (i, h, qi[st], 0)),
        scratch_shapes=[pltpu.VMEM((t, 1), F32), pltpu.VMEM((t, 1), F32), pltpu.VMEM((t, NSA_HD), F32)],
    )
    return pl.pallas_call(
        functools.partial(_slc_kernel, t=t, n_slc=n_slc),
        grid_spec=grid_spec,
        out_shape=jax.ShapeDtypeStruct((b, NSA_HEADS, s, NSA_HD), F32),
        compiler_params=_cparams(("parallel", "parallel", "arbitrary")),
        name="nsa_slc_attention",
    )(jnp.asarray(qi), jnp.asarray(ki), slopes, q, selbias, ks, vs)


def _win_kernel(sl_ref, q_ref, ka_ref, kb_ref, kc_ref, va_ref, vb_ref, vc_ref,
                oc_ref, os_ref, g_ref, o_ref, *, t):
    h = pl.program_id(1)
    qt = pl.program_id(2)
    slope = sl_ref[h]
    q = q_ref[0, 0]
    row = lax.broadcasted_iota(jnp.int32, (t, t), 0)
    col = lax.broadcasted_iota(jnp.int32, (t, t), 1)
    colr = lax.broadcasted_iota(jnp.int32, (1, t), 1)
    nblk = WIN // t
    m = jnp.full((t, 1), NEG, F32)
    l = jnp.zeros((t, 1), F32)
    acc = jnp.zeros((t, NSA_HD), F32)
    refs = ((ka_ref, va_ref), (kb_ref, vb_ref), (kc_ref, vc_ref))
    for j, (k_ref, v_ref) in enumerate(refs):
        off = j - nblk
        s = lax.dot_general(q, k_ref[0, 0], (((1,), (1,)), ((), ())), preferred_element_type=F32)
        s = s + slope * (off * t + colr).astype(F32)
        if off == -nblk:
            mask = (col > row) & (qt + off >= 0)
            s = jnp.where(mask, s, NEG)
        elif off < 0:
            s = jnp.where(qt + off >= 0, s, NEG)
        else:
            s = jnp.where(col <= row, s, NEG)
        m_new = jnp.maximum(m, jnp.max(s, axis=-1, keepdims=True))
        alpha = jnp.exp2(m - m_new)
        p = jnp.exp2(s - m_new)
        l = alpha * l + jnp.sum(p, axis=-1, keepdims=True)
        acc = alpha * acc + jnp.dot(p.astype(BF16), v_ref[0, 0], preferred_element_type=F32)
        m = m_new
    o_win = acc / l
    gt = g_ref[0, 0]
    out = gt[:, 0:1] * oc_ref[0, 0] + gt[:, 1:2] * os_ref[0, 0] + gt[:, 2:3] * o_win
    o_ref[0, 0] = out.astype(o_ref.dtype)


def _win_combine(q, kw, vw, o_cmp, o_slc, gates, *, t=256):
    b, _, s, _ = q.shape
    t = min(t, s)
    assert WIN % t == 0 and WIN // t == 2
    slopes = jnp.asarray(_alibi_slopes_log2(NSA_HEADS))
    kv = lambda off: pl.BlockSpec((1, 1, t, NSA_HD),
                                  lambda i, h, qt, sl: (i, h // HPG, jnp.maximum(qt + off, 0), 0))
    per_head = lambda w: pl.BlockSpec((1, 1, t, w), lambda i, h, qt, sl: (i, h, qt, 0))
    grid_spec = pltpu.PrefetchScalarGridSpec(
        num_scalar_prefetch=1,
        grid=(b, NSA_HEADS, s // t),
        in_specs=[per_head(NSA_HD), kv(-2), kv(-1), kv(0), kv(-2), kv(-1), kv(0),
                  per_head(NSA_HD), per_head(NSA_HD), per_head(3)],
        out_specs=per_head(NSA_HD),
    )
    return pl.pallas_call(
        functools.partial(_win_kernel, t=t),
        grid_spec=grid_spec,
        out_shape=jax.ShapeDtypeStruct((b, NSA_HEADS, s, NSA_HD), BF16),
        compiler_params=_cparams(("parallel", "parallel", "parallel")),
        name="nsa_win_combine",
    )(slopes, q, kw, kw, kw, vw, vw, vw, o_cmp, o_slc, gates)


def _mix_kernel(x_ref, od_ref, on_ref, wg1_ref, wg2_ref, bg1_ref, bg2_ref, wd_ref, wn_ref, o_ref, xb_ref):
    @pl.when(pl.program_id(1) == 0)
    def _():
        xb_ref[...] = x_ref[...].astype(BF16)

    xb = xb_ref[...]
    g1 = jax.nn.sigmoid(jnp.dot(xb, wg1_ref[...], preferred_element_type=F32) + bg1_ref[...])
    g2 = jax.nn.sigmoid(jnp.dot(xb, wg2_ref[...], preferred_element_type=F32) + bg2_ref[...])
    bd = jnp.dot(od_ref[...], wd_ref[...], preferred_element_type=F32)
    bn = jnp.dot(on_ref[...], wn_ref[...], preferred_element_type=F32)
    o_ref[...] = (g1 * bd + g2 * bn).astype(o_ref.dtype)


def _mix(x2, o_diff, o_nsa, w_mgate, b_mgate, w_br_diff, w_br_nsa, *, tm=512, tn=512):
    m, d = x2.shape
    tm = min(tm, m)
    nj = d // tn
    wg = w_mgate.astype(BF16)
    bg = b_mgate.reshape(1, -1).astype(F32)
    row = pl.BlockSpec((tm, d), lambda i, j: (i, 0))
    return pl.pallas_call(
        _mix_kernel,
        grid=(m // tm, nj),
        in_specs=[row, row, row,
                  pl.BlockSpec((d, tn), lambda i, j: (0, j)),
                  pl.BlockSpec((d, tn), lambda i, j: (0, j + nj)),
                  pl.BlockSpec((1, tn), lambda i, j: (0, j)),
                  pl.BlockSpec((1, tn), lambda i, j: (0, j + nj)),
                  pl.BlockSpec((d, tn), lambda i, j: (0, j)),
                  pl.BlockSpec((d, tn), lambda i, j: (0, j))],
        out_specs=pl.BlockSpec((tm, tn), lambda i, j: (i, j)),
        out_shape=jax.ShapeDtypeStruct((m, d), BF16),
        scratch_shapes=[pltpu.VMEM((tm, d), BF16)],
        compiler_params=_cparams(("parallel", "arbitrary")),
        name="branch_mix",
    )(x2, o_diff, o_nsa, wg, wg, bg, bg, w_br_diff.astype(BF16), w_br_nsa.astype(BF16))


def _layer_norm_rows(y, g, b):
    mu = jnp.mean(y, axis=-1, keepdims=True)
    yc = y - mu
    var = jnp.mean(yc * yc, axis=-1, keepdims=True)
    return yc * lax.rsqrt(var + LN_EPS) * g + b


def _split_bf16(a):
    hi = a.astype(BF16)
    lo = (a - hi.astype(F32)).astype(BF16)
    return hi, lo


def _post_attn_kernel(mx_ref, x_ref, wo_ref, g_ref, b_ref, wr_hi_ref, wr_lo_ref, br_ref,
                      h_ref, e_ref, gate_ref, rank_ref, cnt_ref, run_ref, *, tm, alpha):
    i = pl.program_id(0)

    @pl.when(i == 0)
    def _():
        run_ref[...] = jnp.zeros(run_ref.shape, F32)

    y = alpha * x_ref[...] + jnp.dot(mx_ref[...], wo_ref[...], preferred_element_type=F32)
    h = _layer_norm_rows(y, g_ref[...], b_ref[...])
    h_ref[...] = h
    h_hi, h_lo = _split_bf16(h)
    logits = (jnp.dot(h_hi, wr_hi_ref[...], preferred_element_type=F32)
              + jnp.dot(h_hi, wr_lo_ref[...], preferred_element_type=F32)
              + jnp.dot(h_lo, wr_hi_ref[...], preferred_element_type=F32)) + br_ref[...]
    lane = lax.broadcasted_iota(jnp.int32, (tm, N_EXPERTS), 1)
    lane4 = lax.broadcasted_iota(jnp.int32, (tm, TOP_K), 1)
    xl = logits
    vals, idxs, hits = [], [], []
    for _ in range(TOP_K):
        mx = jnp.max(xl, axis=-1, keepdims=True)
        idx = jnp.min(jnp.where(xl == mx, lane, N_EXPERTS), axis=-1, keepdims=True)
        hit = lane == idx
        vals.append(mx)
        idxs.append(idx)
        hits.append(hit)
        xl = jnp.where(hit, LOWEST, xl)
    ex = [jnp.exp(v - vals[0]) for v in vals]
    den = ex[0] + ex[1] + ex[2] + ex[3]
    member = jnp.zeros((tm, N_EXPERTS), F32)
    for hit in hits:
        member = jnp.where(hit, 1.0, member)
    r_i = lax.broadcasted_iota(jnp.int32, (tm, tm), 0)
    c_i = lax.broadcasted_iota(jnp.int32, (tm, tm), 1)
    tri = jnp.where(c_i < r_i, 1.0, 0.0).astype(BF16)
    before = jnp.dot(tri, member.astype(BF16), preferred_element_type=F32) + run_ref[...]
    e_out = jnp.zeros((tm, TOP_K), jnp.int32)
    g_out = jnp.zeros((tm, TOP_K), F32)
    r_out = jnp.zeros((tm, TOP_K), jnp.int32)
    for k in range(TOP_K):
        rk = jnp.sum(jnp.where(hits[k], before, 0.0), axis=-1, keepdims=True)
        e_out = jnp.where(lane4 == k, idxs[k], e_out)
        g_out = jnp.where(lane4 == k, ex[k] / den, g_out)
        r_out = jnp.where(lane4 == k, rk.astype(jnp.int32), r_out)
    e_ref[...] = e_out
    gate_ref[...] = g_out
    rank_ref[...] = r_out
    run_ref[...] = run_ref[...] + jnp.sum(member, axis=0, keepdims=True)
    cnt_ref[...] = run_ref[...].astype(jnp.int32)


def _post_attn(mixed, x2, w_o, ln_g, ln_b, w_router, b_router, *, alpha, tm=256):
    m, d = x2.shape
    tm = min(tm, m)
    wr_hi, wr_lo = _split_bf16(w_router.astype(F32))
    vec = lambda a: a.reshape(1, -1).astype(F32)
    full = lambda r, c: pl.BlockSpec((r, c), lambda i: (0, 0))
    rows = lambda c: pl.BlockSpec((tm, c), lambda i: (i, 0))
    return pl.pallas_call(
        functools.partial(_post_attn_kernel, tm=tm, alpha=alpha),
        grid=(m // tm,),
        in_specs=[rows(d), rows(d), full(d, d), full(1, d), full(1, d),
                  full(d, N_EXPERTS), full(d, N_EXPERTS), full(1, N_EXPERTS)],
        out_specs=[rows(d), rows(TOP_K), rows(TOP_K), rows(TOP_K), full(1, N_EXPERTS)],
        out_shape=[jax.ShapeDtypeStruct((m, d), F32),
                   jax.ShapeDtypeStruct((m, TOP_K), jnp.int32),
                   jax.ShapeDtypeStruct((m, TOP_K), F32),
                   jax.ShapeDtypeStruct((m, TOP_K), jnp.int32),
                   jax.ShapeDtypeStruct((1, N_EXPERTS), jnp.int32)],
        scratch_shapes=[pltpu.VMEM((1, N_EXPERTS), F32)],
        compiler_params=_cparams(("arbitrary",)),
        name="post_attn_router",
    )(mixed, x2, w_o.astype(BF16), vec(ln_g), vec(ln_b), wr_hi, wr_lo, vec(b_router))


def _dest_kernel(ps_ref, e_ref, r_ref, o_ref):
    e = e_ref[...]
    base = jnp.zeros(e.shape, jnp.int32)
    for j in range(N_EXPERTS):
        base = jnp.where(e == j, ps_ref[j], base)
    o_ref[...] = base + r_ref[...]


def _dest(top_e, rank, pstart, *, tm=2048):
    m = top_e.shape[0]
    tm = min(tm, m)
    grid_spec = pltpu.PrefetchScalarGridSpec(
        num_scalar_prefetch=1,
        grid=(m // tm,),
        in_specs=[pl.BlockSpec((tm, TOP_K), lambda i, ps: (i, 0)),
                  pl.BlockSpec((tm, TOP_K), lambda i, ps: (i, 0))],
        out_specs=pl.BlockSpec((tm, TOP_K), lambda i, ps: (i, 0)),
    )
    return pl.pallas_call(
        _dest_kernel, grid_spec=grid_spec,
        out_shape=jax.ShapeDtypeStruct((m, TOP_K), jnp.int32),
        compiler_params=_cparams(("parallel",)),
        name="moe_dest",
    )(pstart, top_e, rank)


def _dispatch_kernel(dest_ref, h_ref, xs_in_ref, xs_ref, sem, *, n_asg):
    del xs_in_ref
    base = pl.program_id(0) * n_asg

    def row_copy(a):
        tok = lax.shift_right_logical(base + a, 2)
        return pltpu.make_async_copy(h_ref.at[pl.ds(tok, 1), :],
                                     xs_ref.at[pl.ds(dest_ref[a], 1), :], sem)

    def start(a, carry):
        row_copy(a).start()
        return carry

    def wait(a, carry):
        row_copy(a).wait()
        return carry

    lax.fori_loop(0, n_asg, start, 0)
    lax.fori_loop(0, n_asg, wait, 0)


def _dispatch(h, dest_flat, n_pad, *, n_asg=1024):
    m, d = h.shape
    total = dest_flat.shape[0]
    n_asg = min(n_asg, total)
    assert TOP_K == 4 and total % n_asg == 0
    xs0 = jnp.zeros((n_pad, d), h.dtype)
    return pl.pallas_call(
        functools.partial(_dispatch_kernel, n_asg=n_asg),
        grid=(total // n_asg,),
        in_specs=[pl.BlockSpec((n_asg,), lambda i: (i,), memory_space=pltpu.SMEM),
                  pl.BlockSpec(memory_space=pl.ANY),
                  pl.BlockSpec(memory_space=pl.ANY)],
        out_specs=pl.BlockSpec(memory_space=pl.ANY),
        out_shape=jax.ShapeDtypeStruct((n_pad, d), h.dtype),
        scratch_shapes=[pltpu.SemaphoreType.DMA(())],
        input_output_aliases={2: 0},
        compiler_params=_cparams(("arbitrary",)),
        name="moe_dispatch",
    )(dest_flat, h, xs0)


def _expert_kernel(be_ref, nu_ref, xs_ref, w1_ref, b1_ref, w2_ref, b2_ref, y_ref):
    i = pl.program_id(0)

    @pl.when(i < nu_ref[0])
    def _():
        xb = xs_ref[...].astype(BF16)
        hb = jnp.dot(xb, w1_ref[0], preferred_element_type=F32) + b1_ref[0]
        glu = jnp.minimum(hb[:, :D_FF], SWIGLU_LIMIT)
        lin = jnp.clip(hb[:, D_FF:], -SWIGLU_LIMIT, SWIGLU_LIMIT)
        act = glu * jax.nn.sigmoid(SWIGLU_ALPHA * glu) * (lin + 1.0)
        y_ref[...] = jnp.dot(act.astype(BF16), w2_ref[0], preferred_element_type=F32) + b2_ref[0]

    @pl.when(i >= nu_ref[0])
    def _():
        y_ref[...] = jnp.zeros(y_ref.shape, y_ref.dtype)


def _experts(xs, blk_expert, n_used, w1, b1, w2, b2):
    n_pad, d = xs.shape
    n_blocks = n_pad // MOE_BLOCK
    last = lambda i, nu: jnp.minimum(i, nu[0] - 1)
    grid_spec = pltpu.PrefetchScalarGridSpec(
        num_scalar_prefetch=2,
        grid=(n_blocks,),
        in_specs=[pl.BlockSpec((MOE_BLOCK, d), lambda i, be, nu: (last(i, nu), 0)),
                  pl.BlockSpec((1, d, 2 * D_FF), lambda i, be, nu: (be[last(i, nu)], 0, 0)),
                  pl.BlockSpec((1, 1, 2 * D_FF), lambda i, be, nu: (be[last(i, nu)], 0, 0)),
                  pl.BlockSpec((1, D_FF, d), lambda i, be, nu: (be[last(i, nu)], 0, 0)),
                  pl.BlockSpec((1, 1, d), lambda i, be, nu: (be[last(i, nu)], 0, 0))],
        out_specs=pl.BlockSpec((MOE_BLOCK, d), lambda i, be, nu: (i, 0)),
    )
    return pl.pallas_call(
        _expert_kernel, grid_spec=grid_spec,
        out_shape=jax.ShapeDtypeStruct((n_pad, d), F32),
        compiler_params=_cparams(("arbitrary",)),
        name="moe_experts",
    )(blk_expert, n_used, xs, w1.astype(BF16), b1.reshape(N_EXPERTS, 1, -1).astype(F32),
      w2.astype(BF16), b2.reshape(N_EXPERTS, 1, -1).astype(F32))


def _final_kernel(dest_ref, h_ref, gate_ref, p_ref, wpg_ref, bpg_ref, wpp_ref, g_ref, b_ref, y_ref,
                  o_ref, buf, sem, *, tm, alpha):
    def row_copy(r, k):
        return pltpu.make_async_copy(y_ref.at[pl.ds(dest_ref[r * TOP_K + k], 1), :],
                                     buf.at[k, pl.ds(r, 1), :], sem)

    def start(r, carry):
        for k in range(TOP_K):
            row_copy(r, k).start()
        return carry

    def wait(r, carry):
        for k in range(TOP_K):
            row_copy(r, k).wait()
        return carry

    lax.fori_loop(0, tm, start, 0)
    h = h_ref[...]
    ple = (jax.nn.sigmoid(jnp.dot(h.astype(BF16), wpg_ref[...], preferred_element_type=F32) + bpg_ref[...])
           * jnp.dot(p_ref[...].astype(BF16), wpp_ref[...], preferred_element_type=F32))
    lax.fori_loop(0, tm, wait, 0)
    gate = gate_ref[...]
    ffn = gate[:, 0:1] * buf[0]
    for k in range(1, TOP_K):
        ffn = ffn + gate[:, k:k + 1] * buf[k]
    o_ref[...] = _layer_norm_rows(alpha * h + ffn + ple, g_ref[...], b_ref[...])


def _final(h, gate, dest_flat, y, p2, w_ple_gate, b_ple_gate, w_ple_proj, ln_g, ln_b, *, alpha, tm=256):
    m, d = h.shape
    tm = min(tm, m)
    vec = lambda a: a.reshape(1, -1).astype(F32)
    full = lambda r, c: pl.BlockSpec((r, c), lambda i: (0, 0))
    rows = lambda c: pl.BlockSpec((tm, c), lambda i: (i, 0))
    return pl.pallas_call(
        functools.partial(_final_kernel, tm=tm, alpha=alpha),
        grid=(m // tm,),
        in_specs=[pl.BlockSpec((tm * TOP_K,), lambda i: (i,), memory_space=pltpu.SMEM),
                  rows(d), rows(TOP_K), rows(PLE_DIM), full(d, d), full(1, d), full(PLE_DIM, d),
                  full(1, d), full(1, d), pl.BlockSpec(memory_space=pl.ANY)],
        out_specs=rows(d),
        out_shape=jax.ShapeDtypeStruct((m, d), F32),
        scratch_shapes=[pltpu.VMEM((TOP_K, tm, d), F32), pltpu.SemaphoreType.DMA(())],
        compiler_params=_cparams(("arbitrary",)),
        name="moe_combine_final",
    )(dest_flat, h, gate, p2, w_ple_gate.astype(BF16), vec(b_ple_gate), w_ple_proj.astype(BF16),
      vec(ln_g), vec(ln_b), y)


def _attention_block(x2, b, s, w_in, diff_p, nsa_p, lambda_init):
    t = b * s
    qscale = DIFF_HD ** -0.5 * LOG2E
    scale = np.ones((1, C_MAIN), np.float32)
    scale[0, :DIFF_QK] = qscale
    o_nq = 2 * DIFF_QK + DIFF_V
    scale[0, o_nq:o_nq + NSA_Q] = NSA_HD ** -0.5 * LOG2E
    proj = _matmul(x2, w_in[:, :C_MAIN].astype(BF16), jnp.asarray(scale),
                   jnp.zeros((1, C_MAIN), F32), out_dtype=BF16)
    wg = jnp.pad(w_in[:, C_MAIN:], ((0, 0), (0, 128 - NSA_GATES))).astype(BF16)
    gates = _matmul(x2, wg, jnp.ones((1, 128), F32), jnp.zeros((1, 128), F32),
                    act="sigmoid", out_dtype=F32)[:, :NSA_GATES]

    offs = np.cumsum((0,) + IN_SIZES[:-1])
    col = lambda i: proj[:, offs[i]:offs[i] + IN_SIZES[i]]
    dq = col(0).reshape(b, s, DIFF_HEADS * 2, DIFF_HD).transpose(0, 2, 1, 3)
    dk = col(1).reshape(b, s, DIFF_HEADS * 2, DIFF_HD).transpose(0, 2, 1, 3)
    dv = col(2).reshape(b, s, DIFF_HEADS, 2 * DIFF_HD).transpose(0, 2, 1, 3)
    o_diff = _diff_attention(dq, dk, dv, *diff_p, lambda_init)

    nq = col(3).reshape(b, s, NSA_HEADS, NSA_HD).transpose(0, 2, 1, 3)
    grp = lambda i: col(i).reshape(b, s, NSA_KV, NSA_HD).transpose(0, 2, 1, 3)
    halves = lambda i: (col(i).reshape(b, s // CMP_STRIDE, CMP_STRIDE, NSA_KV, NSA_HD)
                        .transpose(0, 3, 1, 2, 4).reshape(b, NSA_KV, s // CMP_STRIDE, CMP_STRIDE * NSA_HD))
    pos_k, pos_v, phi_k1, phi_k2, phi_v1, phi_v2 = nsa_p
    k_cmp = _compress(halves(4), pos_k, phi_k1, phi_k2)
    v_cmp = _compress(halves(5), pos_v, phi_v1, phi_v2)
    o_cmp, selbias = _cmp_select(nq, k_cmp, v_cmp)
    o_slc = _slc_attention(nq, selbias, grp(6), grp(7))
    g4 = gates.reshape(b, s, NSA_HEADS, 3).transpose(0, 2, 1, 3)
    o_nsa = _win_combine(nq, grp(8), grp(9), o_cmp, o_slc, g4)
    o_nsa = o_nsa.transpose(0, 2, 1, 3).reshape(t, NSA_Q)
    return o_diff.reshape(t, DIFF_V), o_nsa


def _moe_block(h, top_e, gate, rank, counts, w1, b1, w2, b2):
    t = h.shape[0]
    n_blocks = (t * TOP_K + MOE_BLOCK - 1) // MOE_BLOCK + N_EXPERTS
    n_pad = n_blocks * MOE_BLOCK
    counts = counts.reshape(N_EXPERTS)
    pcounts = (counts + MOE_BLOCK - 1) // MOE_BLOCK * MOE_BLOCK
    pend = jnp.cumsum(pcounts)
    pstart = (pend - pcounts).astype(jnp.int32)
    blk_expert = jnp.minimum(
        jnp.searchsorted(pend, jnp.arange(n_blocks, dtype=jnp.int32) * MOE_BLOCK, side='right'),
        N_EXPERTS - 1).astype(jnp.int32)
    n_used = (pend[-1:] // MOE_BLOCK).astype(jnp.int32)
    dest = _dest(top_e, rank, pstart).reshape(t * TOP_K)
    xs = _dispatch(h, dest, n_pad)
    y = _experts(xs, blk_expert, n_used, w1, b1, w2, b2)
    return dest, y


def kernel(x, p, w_in, diff_lq1, diff_lk1, diff_lq2, diff_lk2, diff_subln_g, nsa_pos_k, nsa_pos_v,
           nsa_phi_k1, nsa_phi_k2, nsa_phi_v1, nsa_phi_v2, w_br_diff, w_br_nsa, w_mgate, b_mgate, w_o,
           ln1_g, ln1_b, w_router, b_router, w_e1, b_e1, w_e2, b_e2, w_ple_gate, b_ple_gate,
           w_ple_proj, ln2_g, ln2_b):
    b, s, d = x.shape
    depth = w_in.shape[0]
    alpha = (2.0 * depth) ** 0.25
    x2 = x.reshape(b * s, d)
    for i in range(depth):
        lambda_init = 0.8 - 0.6 * math.exp(-0.3 * i)
        o_diff, o_nsa = _attention_block(
            x2, b, s, w_in[i],
            (diff_lq1[i], diff_lk1[i], diff_lq2[i], diff_lk2[i], diff_subln_g[i]),
            (nsa_pos_k[i], nsa_pos_v[i], nsa_phi_k1[i], nsa_phi_k2[i], nsa_phi_v1[i], nsa_phi_v2[i]),
            lambda_init)
        mixed = _mix(x2, o_diff, o_nsa, w_mgate[i], b_mgate[i], w_br_diff[i], w_br_nsa[i])
        h, top_e, gate, rank, counts = _post_attn(mixed, x2, w_o[i], ln1_g[i], ln1_b[i],
                                                  w_router[i], b_router[i], alpha=alpha)
        dest, y = _moe_block(h, top_e, gate, rank, counts, w_e1[i], b_e1[i], w_e2[i], b_e2[i])
        x2 = _final(h, gate, dest, y, p[i].reshape(b * s, -1), w_ple_gate[i], b_ple_gate[i],
                    w_ple_proj[i], ln2_g[i], ln2_b[i], alpha=alpha)
    return x2.reshape(b, s, d)
```

```python
import functools
import math

import numpy as np
import jax
import jax.numpy as jnp
from jax import lax
from jax.experimental import pallas as pl
from jax.experimental.pallas import tpu as pltpu

D_MODEL = 1024
DIFF_HEADS = 8
DIFF_HD = 64
NSA_HEADS = 16
NSA_KV = 4
NSA_HD = 64
HPG = NSA_HEADS // NSA_KV
CMP_LEN = 32
CMP_STRIDE = 16
SLC_LEN = 64
SLC_SHIFT = 6
SLC_TOPN = 16
WIN = 512
PHI_HID = 256
N_EXPERTS = 32
TOP_K = 4
D_FF = 1024
SWIGLU_LIMIT = 7.0
SWIGLU_ALPHA = 1.702
MOE_BLOCK = 512
PLE_DIM = 256
LN_EPS = 1e-5
NEG = -1e30
BIG = 1e30
LOWEST = -3e38
LOG2E = 1.4426950408889634

DIFF_QK = DIFF_HEADS * 2 * DIFF_HD
DIFF_V = DIFF_HEADS * 2 * DIFF_HD
NSA_Q = NSA_HEADS * NSA_HD
NSA_KVW = NSA_KV * NSA_HD
NSA_GATES = NSA_HEADS * 3
IN_SIZES = (DIFF_QK, DIFF_QK, DIFF_V, NSA_Q) + (NSA_KVW,) * 6 + (NSA_GATES,)
C_MAIN = sum(IN_SIZES[:-1])

VMEM_LIMIT = 48 * 1024 * 1024

BF16 = jnp.bfloat16
F32 = jnp.float32


def _alibi_slopes_log2(n):
    return (2.0 ** (-8.0 * np.arange(1, n + 1) / n) * LOG2E).astype(np.float32)


def _cparams(sem, vmem=VMEM_LIMIT):
    return pltpu.CompilerParams(dimension_semantics=sem, vmem_limit_bytes=vmem)


def _mm_kernel(x_ref, w_ref, s_ref, b_ref, o_ref, xb_ref, *, act):
    @pl.when(pl.program_id(1) == 0)
    def _():
        xb_ref[...] = x_ref[...].astype(BF16)

    acc = jnp.dot(xb_ref[...], w_ref[...], preferred_element_type=F32)
    acc = acc * s_ref[...] + b_ref[...]
    if act == "sigmoid":
        acc = jax.nn.sigmoid(acc)
    o_ref[...] = acc.astype(o_ref.dtype)


def _matmul(x, w, scale, bias, *, act=None, out_dtype=F32, tm=512, tn=512):
    m, k = x.shape
    n = w.shape[1]
    tm = min(tm, m)
    tn = min(tn, n)
    assert m % tm == 0 and n % tn == 0
    return pl.pallas_call(
        functools.partial(_mm_kernel, act=act),
        grid=(m // tm, n // tn),
        in_specs=[
            pl.BlockSpec((tm, k), lambda i, j: (i, 0)),
            pl.BlockSpec((k, tn), lambda i, j: (0, j)),
            pl.BlockSpec((1, tn), lambda i, j: (0, j)),
            pl.BlockSpec((1, tn), lambda i, j: (0, j)),
        ],
        out_specs=pl.BlockSpec((tm, tn), lambda i, j: (i, j)),
        out_shape=jax.ShapeDtypeStruct((m, n), out_dtype),
        scratch_shapes=[pltpu.VMEM((tm, k), BF16)],
        compiler_params=_cparams(("parallel", "arbitrary")),
        name="dense_matmul",
    )(x, w, scale, bias)


LANES = 128


def _online_update(s, v, m_old, l_old, acc_old):
    tk = s.shape[1]
    hd = acc_old.shape[1]
    m_new = jnp.maximum(m_old, jnp.max(s, axis=-1, keepdims=True))
    alpha = jnp.exp2(m_old - m_new)
    p = jnp.exp2(s - jnp.concatenate([m_new] * (tk // LANES), axis=1))
    psum = p[:, :LANES]
    for c in range(1, tk // LANES):
        psum = psum + p[:, c * LANES:(c + 1) * LANES]
    l_new = alpha * l_old + psum
    acc_new = alpha[:, :hd] * acc_old + jnp.dot(p.astype(BF16), v, preferred_element_type=F32)
    return m_new, l_new, acc_new


def _online_step(s, v, m_ref, l_ref, acc_ref):
    m_new, l_new, acc_new = _online_update(s, v, m_ref[...], l_ref[...], acc_ref[...])
    m_ref[...] = m_new
    l_ref[...] = l_new
    acc_ref[...] = acc_new


def _row_total(l):
    return jnp.sum(l, axis=-1, keepdims=True)


def _tri_steps(n):
    qi, ki = [], []
    for q in range(n):
        for k in range(q + 1):
            qi.append(q)
            ki.append(k)
    return np.asarray(qi, np.int32), np.asarray(ki, np.int32)


def _diff_kernel(qi_ref, ki_ref, sl_ref, q_ref, k_ref, v_ref,
                 lq1_ref, lk1_ref, lq2_ref, lk2_ref, g_ref, o_ref,
                 qm0, qm1, m0, l0, a0, m1, l1, a1, *, t, lambda_init):
    h = pl.program_id(1)
    step = pl.program_id(2)
    qt = qi_ref[step]
    kt = ki_ref[step]
    slope = sl_ref[h]

    @pl.when(kt == 0)
    def _():
        q = q_ref[0]
        lane = lax.broadcasted_iota(jnp.int32, q.shape, 1)
        qm0[...] = jnp.where(lane < DIFF_HD, q, jnp.zeros_like(q))
        qm1[...] = jnp.where(lane >= DIFF_HD, q, jnp.zeros_like(q))
        for m, l, a in ((m0, l0, a0), (m1, l1, a1)):
            m[...] = jnp.full(m.shape, NEG, F32)
            l[...] = jnp.zeros(l.shape, F32)
            a[...] = jnp.zeros(a.shape, F32)

    col = lax.broadcasted_iota(jnp.int32, (1, t), 1)
    bias = slope * ((kt - qt) * t + col).astype(F32)
    k = k_ref[0]
    v = v_ref[0]

    def scores(qm):
        return lax.dot_general(qm[...], k, (((1,), (1,)), ((), ())), preferred_element_type=F32)

    @pl.when(kt < qt)
    def _():
        _online_step(scores(qm0) + bias, v, m0, l0, a0)
        _online_step(scores(qm1) + bias, v, m1, l1, a1)

    @pl.when(kt == qt)
    def _():
        row = lax.broadcasted_iota(jnp.int32, (t, t), 0)
        colf = lax.broadcasted_iota(jnp.int32, (t, t), 1)
        bias_c = jnp.where(colf <= row, bias, NEG)
        _online_step(scores(qm0) + bias_c, v, m0, l0, a0)
        _online_step(scores(qm1) + bias_c, v, m1, l1, a1)
        lam = (jnp.exp(jnp.sum(lq1_ref[...] * lk1_ref[...], axis=-1, keepdims=True))
               - jnp.exp(jnp.sum(lq2_ref[...] * lk2_ref[...], axis=-1, keepdims=True))
               + lambda_init)
        o = a0[...] / _row_total(l0[...]) - lam * (a1[...] / _row_total(l1[...]))
        ms = jnp.mean(o * o, axis=-1, keepdims=True)
        o = o * lax.rsqrt(ms + LN_EPS) * g_ref[...] * (1.0 - lambda_init)
        o_ref[0] = o.astype(o_ref.dtype)


def _diff_attention(proj3, lq1, lk1, lq2, lk2, g, lambda_init, *, t=512):
    b, s, _ = proj3.shape
    t = min(t, s)
    qi, ki = _tri_steps(s // t)
    w = 2 * DIFF_HD
    slopes = jnp.asarray(_alibi_slopes_log2(DIFF_HEADS))
    vec = lambda a: a.reshape(1, -1).astype(F32)
    small = lambda c: pl.BlockSpec((1, c), lambda b_, h, st, qi, ki, sl: (0, 0))
    grid_spec = pltpu.PrefetchScalarGridSpec(
        num_scalar_prefetch=3,
        grid=(b, DIFF_HEADS, len(qi)),
        in_specs=[pl.BlockSpec((1, t, w), lambda b_, h, st, qi, ki, sl: (b_, qi[st], h)),
                  pl.BlockSpec((1, t, w), lambda b_, h, st, qi, ki, sl: (b_, ki[st], DIFF_HEADS + h)),
                  pl.BlockSpec((1, t, w), lambda b_, h, st, qi, ki, sl: (b_, ki[st], 2 * DIFF_HEADS + h)),
                  small(DIFF_HD), small(DIFF_HD), small(DIFF_HD), small(DIFF_HD), small(w)],
        out_specs=pl.BlockSpec((1, t, w), lambda b_, h, st, qi, ki, sl: (b_, qi[st], h)),
        scratch_shapes=[pltpu.VMEM((t, w), BF16), pltpu.VMEM((t, w), BF16),
                        pltpu.VMEM((t, LANES), F32), pltpu.VMEM((t, LANES), F32), pltpu.VMEM((t, w), F32),
                        pltpu.VMEM((t, LANES), F32), pltpu.VMEM((t, LANES), F32), pltpu.VMEM((t, w), F32)],
    )
    return pl.pallas_call(
        functools.partial(_diff_kernel, t=t, lambda_init=lambda_init),
        grid_spec=grid_spec,
        out_shape=jax.ShapeDtypeStruct((b, s, DIFF_V), BF16),
        compiler_params=_cparams(("parallel", "parallel", "arbitrary")),
        name="diff_attention",
    )(jnp.asarray(qi), jnp.asarray(ki), slopes, proj3, proj3, proj3,
      vec(lq1), vec(lk1), vec(lq2), vec(lk2), vec(g))


def _compress_kernel(a_ref, w1_ref, w2_ref, pos_ref, o_ref):
    a = a_ref[0, 0]
    half = CMP_STRIDE * NSA_HD
    top = jnp.dot(a, w1_ref[:half, :], preferred_element_type=F32)
    bot = jnp.dot(a, w1_ref[half:, :], preferred_element_type=F32)
    pos = pos_ref[...]
    pos_hi = pos.astype(BF16)
    pos_lo = (pos - pos_hi.astype(F32)).astype(BF16)
    c = (jnp.dot(pos_hi, w1_ref[...], preferred_element_type=F32)
         + jnp.dot(pos_lo, w1_ref[...], preferred_element_type=F32))[0:1]
    n = a.shape[0]
    hid = top + pltpu.roll(bot, n - 1, axis=0) + c
    hid = hid * jax.nn.sigmoid(hid)
    o_ref[0, 0] = jnp.dot(hid.astype(BF16), w2_ref[...], preferred_element_type=F32).astype(o_ref.dtype)


def _compress(t_halves, pos, w1, w2):
    b, g, nh, _ = t_halves.shape
    pos8 = jnp.broadcast_to(pos.reshape(1, -1).astype(F32), (8, CMP_LEN * NSA_HD))
    return pl.pallas_call(
        _compress_kernel,
        grid=(b, g),
        in_specs=[pl.BlockSpec((1, 1, nh, CMP_STRIDE * NSA_HD), lambda i, j: (i, j, 0, 0)),
                  pl.BlockSpec((CMP_LEN * NSA_HD, PHI_HID), lambda i, j: (0, 0)),
                  pl.BlockSpec((PHI_HID, NSA_HD), lambda i, j: (0, 0)),
                  pl.BlockSpec((8, CMP_LEN * NSA_HD), lambda i, j: (0, 0))],
        out_specs=pl.BlockSpec((1, 1, nh, NSA_HD), lambda i, j: (i, j, 0, 0)),
        out_shape=jax.ShapeDtypeStruct((b, g, nh, NSA_HD), BF16),
        compiler_params=_cparams(("parallel", "parallel")),
        name="nsa_compress",
    )(t_halves, w1.astype(BF16), w2.astype(BF16), pos8)


def _cmp_select_kernel(sl_ref, q_ref, kc_ref, vc_ref, o_ref, sb_ref, *, tq, n_cmp_pad, n_slc, n_top):
    g = pl.program_id(1)
    qt = pl.program_id(2)
    t_col = qt * tq + lax.broadcasted_iota(jnp.int32, (tq, 1), 0)
    cmp_pos = lax.broadcasted_iota(jnp.int32, (1, n_cmp_pad), 1) * CMP_STRIDE + (CMP_LEN - 1)
    dist = t_col - cmp_pos
    valid = dist >= 0
    distf = dist.astype(F32)
    kc = kc_ref[0, 0]
    vc = vc_ref[0, 0]
    psum = jnp.zeros((tq, n_cmp_pad), F32)
    for hh in range(HPG):
        slope = sl_ref[g * HPG + hh]
        s = lax.dot_general(q_ref[0, hh], kc, (((1,), (1,)), ((), ())), preferred_element_type=F32)
        s = jnp.where(valid, s - slope * distf, NEG)
        m = jnp.max(s, axis=-1, keepdims=True)
        p = jnp.where(valid, jnp.exp2(s - m), 0.0)
        l = jnp.maximum(jnp.sum(p, axis=-1, keepdims=True), 1e-30)
        p = p / l
        o_ref[0, hh] = jnp.dot(p.astype(BF16), vc, preferred_element_type=F32)
        psum = psum + p
    c_lo = lax.broadcasted_iota(jnp.int32, (n_cmp_pad, 128), 0) * CMP_STRIDE
    s_lo = lax.broadcasted_iota(jnp.int32, (n_cmp_pad, 128), 1) * SLC_LEN
    ov = jnp.minimum(c_lo + CMP_LEN, s_lo + SLC_LEN) - jnp.maximum(c_lo, s_lo)
    cmap = (jnp.maximum(ov, 0).astype(F32) * (1.0 / CMP_LEN)).astype(BF16)
    ps_hi = psum.astype(BF16)
    ps_lo = (psum - ps_hi.astype(F32)).astype(BF16)
    imp = (jnp.dot(ps_hi, cmap, preferred_element_type=F32)
           + jnp.dot(ps_lo, cmap, preferred_element_type=F32))
    imp_t = jnp.transpose(imp)[:n_slc]
    blk = lax.broadcasted_iota(jnp.int32, (n_slc, tq), 0)
    t_row = qt * tq + lax.broadcasted_iota(jnp.int32, (n_slc, tq), 1)
    cur = lax.shift_right_logical(t_row, SLC_SHIFT)
    forced = (blk == 0) | (blk == cur) | (blk == cur - 1)
    x = jnp.where(forced, BIG, jnp.where(blk <= cur, imp_t, NEG))
    sel = jnp.zeros((n_slc, tq), F32)
    for _ in range(n_top):
        mx = jnp.max(x, axis=0, keepdims=True)
        idx = jnp.min(jnp.where(x == mx, blk, n_slc), axis=0, keepdims=True)
        hit = blk == idx
        sel = jnp.where(hit, 1.0, sel)
        x = jnp.where(hit, LOWEST, x)
    if n_slc < 128:
        sel = jnp.concatenate([sel, jnp.zeros((128 - n_slc, tq), F32)], axis=0)
    sel_q = jnp.transpose(sel)[:, :n_slc]
    sb_ref[0, 0] = jnp.where(sel_q > 0.5, 0.0, NEG).astype(sb_ref.dtype)


def _cmp_select(q, k_cmp, v_cmp, *, tq=512):
    b, _, s, _ = q.shape
    tq = min(tq, s)
    n_cmp_pad = k_cmp.shape[2]
    n_slc = s // SLC_LEN
    n_top = min(SLC_TOPN, n_slc)
    slopes = jnp.asarray(_alibi_slopes_log2(NSA_HEADS))
    grid_spec = pltpu.PrefetchScalarGridSpec(
        num_scalar_prefetch=1,
        grid=(b, NSA_KV, s // tq),
        in_specs=[pl.BlockSpec((1, HPG, tq, NSA_HD), lambda i, j, k, sl: (i, j, k, 0)),
                  pl.BlockSpec((1, 1, n_cmp_pad, NSA_HD), lambda i, j, k, sl: (i, j, 0, 0)),
                  pl.BlockSpec((1, 1, n_cmp_pad, NSA_HD), lambda i, j, k, sl: (i, j, 0, 0))],
        out_specs=[pl.BlockSpec((1, HPG, tq, NSA_HD), lambda i, j, k, sl: (i, j, k, 0)),
                   pl.BlockSpec((1, 1, tq, n_slc), lambda i, j, k, sl: (i, j, k, 0))],
    )
    return pl.pallas_call(
        functools.partial(_cmp_select_kernel, tq=tq, n_cmp_pad=n_cmp_pad, n_slc=n_slc, n_top=n_top),
        grid_spec=grid_spec,
        out_shape=[jax.ShapeDtypeStruct((b, NSA_HEADS, s, NSA_HD), F32),
                   jax.ShapeDtypeStruct((b, NSA_KV, s, n_slc), BF16)],
        compiler_params=_cparams(("parallel", "parallel", "parallel")),
        name="nsa_cmp_select",
    )(slopes, q, k_cmp, v_cmp)


def _slc_kernel(qi_ref, ki_ref, sl_ref, q_ref, sb_ref, k_ref, v_ref, o_ref, m_sc, l_sc, a_sc, *, t, n_slc):
    g = pl.program_id(1)
    step = pl.program_id(2)
    qt = qi_ref[step]
    kt = ki_ref[step]

    @pl.when(kt == 0)
    def _():
        m_sc[...] = jnp.full(m_sc.shape, NEG, F32)
        l_sc[...] = jnp.zeros(l_sc.shape, F32)
        a_sc[...] = jnp.zeros(a_sc.shape, F32)

    col = lax.broadcasted_iota(jnp.int32, (1, t), 1)
    rel = ((kt - qt) * t + col).astype(F32)
    blk_of_key = lax.shift_right_logical(kt * t + lax.broadcasted_iota(jnp.int32, (n_slc, t), 1), SLC_SHIFT)
    onehot = jnp.where(blk_of_key == lax.broadcasted_iota(jnp.int32, (n_slc, t), 0), 1.0, 0.0).astype(BF16)
    sel = jnp.dot(sb_ref[0, 0], onehot, preferred_element_type=F32)
    k = k_ref[0, 0]
    v = v_ref[0, 0]

    def sweep(sel_bias):
        for hh in range(HPG):
            s = lax.dot_general(q_ref[0, hh], k, (((1,), (1,)), ((), ())), preferred_element_type=F32)
            s = s + (sel_bias + sl_ref[g * HPG + hh] * rel)
            m_new, l_new, a_new = _online_update(s, v, m_sc[hh], l_sc[hh], a_sc[hh])
            m_sc[hh] = m_new
            l_sc[hh] = l_new
            a_sc[hh] = a_new

    @pl.when(kt < qt)
    def _():
        sweep(sel)

    @pl.when(kt == qt)
    def _():
        row = lax.broadcasted_iota(jnp.int32, (t, t), 0)
        colf = lax.broadcasted_iota(jnp.int32, (t, t), 1)
        sweep(jnp.where(colf <= row, sel, NEG))
        for hh in range(HPG):
            o_ref[0, hh] = a_sc[hh] / _row_total(l_sc[hh])


def _slc_attention(q, selbias, ks, vs, *, t=512):
    b, _, s, _ = q.shape
    t = min(t, s)
    n_slc = selbias.shape[-1]
    qi, ki = _tri_steps(s // t)
    slopes = jnp.asarray(_alibi_slopes_log2(NSA_HEADS))
    heads = pl.BlockSpec((1, HPG, t, NSA_HD), lambda i, g, st, qi, ki, sl: (i, g, qi[st], 0))
    kv = pl.BlockSpec((1, 1, t, NSA_HD), lambda i, g, st, qi, ki, sl: (i, g, ki[st], 0))
    grid_spec = pltpu.PrefetchScalarGridSpec(
        num_scalar_prefetch=3,
        grid=(b, NSA_KV, len(qi)),
        in_specs=[heads,
                  pl.BlockSpec((1, 1, t, n_slc), lambda i, g, st, qi, ki, sl: (i, g, qi[st], 0)),
                  kv, kv],
        out_specs=heads,
        scratch_shapes=[pltpu.VMEM((HPG, t, LANES), F32), pltpu.VMEM((HPG, t, LANES), F32),
                        pltpu.VMEM((HPG, t, NSA_HD), F32)],
    )
    return pl.pallas_call(
        functools.partial(_slc_kernel, t=t, n_slc=n_slc),
        grid_spec=grid_spec,
        out_shape=jax.ShapeDtypeStruct((b, NSA_HEADS, s, NSA_HD), F32),
        compiler_params=_cparams(("parallel", "parallel", "arbitrary")),
        name="nsa_slc_attention",
    )(jnp.asarray(qi), jnp.asarray(ki), slopes, q, selbias, ks, vs)


def _win_kernel(sl_ref, q_ref, ka_ref, kb_ref, kc_ref, va_ref, vb_ref, vc_ref,
                oc_ref, os_ref, g_ref, o_ref, *, t):
    g = pl.program_id(1)
    qt = pl.program_id(2)
    row = lax.broadcasted_iota(jnp.int32, (t, t), 0)
    col = lax.broadcasted_iota(jnp.int32, (t, t), 1)
    colr = lax.broadcasted_iota(jnp.int32, (1, t), 1)
    nblk = WIN // t
    refs = ((ka_ref, va_ref), (kb_ref, vb_ref), (kc_ref, vc_ref))
    masks = []
    for j in range(len(refs)):
        off = j - nblk
        if off == -nblk:
            masks.append(jnp.where((col > row) & (qt + off >= 0), 0.0, NEG))
        elif off < 0:
            masks.append(jnp.where(qt + off >= 0, 0.0, NEG))
        else:
            masks.append(jnp.where(col <= row, 0.0, NEG))
    for hh in range(HPG):
        slope = sl_ref[g * HPG + hh]
        q = q_ref[0, hh]
        m = jnp.full((t, LANES), NEG, F32)
        l = jnp.zeros((t, LANES), F32)
        acc = jnp.zeros((t, NSA_HD), F32)
        for j, (k_ref, v_ref) in enumerate(refs):
            off = j - nblk
            s = lax.dot_general(q, k_ref[0, 0], (((1,), (1,)), ((), ())), preferred_element_type=F32)
            s = s + (masks[j] + slope * (off * t + colr).astype(F32))
            m, l, acc = _online_update(s, v_ref[0, 0], m, l, acc)
        o_win = acc / _row_total(l)
        gt = g_ref[0, hh]
        out = gt[:, 0:1] * oc_ref[0, hh] + gt[:, 1:2] * os_ref[0, hh] + gt[:, 2:3] * o_win
        o_ref[0, hh] = out.astype(o_ref.dtype)


def _win_combine(q, kw, vw, o_cmp, o_slc, gates, *, t=256):
    b, _, s, _ = q.shape
    t = min(t, s)
    assert WIN % t == 0 and WIN // t == 2
    slopes = jnp.asarray(_alibi_slopes_log2(NSA_HEADS))
    kv = lambda off: pl.BlockSpec((1, 1, t, NSA_HD),
                                  lambda i, g, qt, sl: (i, g, jnp.maximum(qt + off, 0), 0))
    per_head = lambda w: pl.BlockSpec((1, HPG, t, w), lambda i, g, qt, sl: (i, g, qt, 0))
    grid_spec = pltpu.PrefetchScalarGridSpec(
        num_scalar_prefetch=1,
        grid=(b, NSA_KV, s // t),
        in_specs=[per_head(NSA_HD), kv(-2), kv(-1), kv(0), kv(-2), kv(-1), kv(0),
                  per_head(NSA_HD), per_head(NSA_HD), per_head(3)],
        out_specs=per_head(NSA_HD),
    )
    return pl.pallas_call(
        functools.partial(_win_kernel, t=t),
        grid_spec=grid_spec,
        out_shape=jax.ShapeDtypeStruct((b, NSA_HEADS, s, NSA_HD), BF16),
        compiler_params=_cparams(("parallel", "parallel", "parallel")),
        name="nsa_win_combine",
    )(slopes, q, kw, kw, kw, vw, vw, vw, o_cmp, o_slc, gates)


def _mix_kernel(x_ref, od_ref, on_ref, wg1_ref, wg2_ref, bg1_ref, bg2_ref, wd_ref, wn_ref, o_ref, xb_ref):
    @pl.when(pl.program_id(1) == 0)
    def _():
        xb_ref[...] = x_ref[...].astype(BF16)

    xb = xb_ref[...]
    g1 = jax.nn.sigmoid(jnp.dot(xb, wg1_ref[...], preferred_element_type=F32) + bg1_ref[...])
    g2 = jax.nn.sigmoid(jnp.dot(xb, wg2_ref[...], preferred_element_type=F32) + bg2_ref[...])
    bd = jnp.dot(od_ref[...], wd_ref[...], preferred_element_type=F32)
    bn = jnp.dot(on_ref[...], wn_ref[...], preferred_element_type=F32)
    o_ref[...] = (g1 * bd + g2 * bn).astype(o_ref.dtype)


def _mix(x2, o_diff, o_nsa, w_mgate, b_mgate, w_br_diff, w_br_nsa, *, tm=512, tn=512):
    m, d = x2.shape
    tm = min(tm, m)
    nj = d // tn
    wg = w_mgate.astype(BF16)
    bg = b_mgate.reshape(1, -1).astype(F32)
    row = pl.BlockSpec((tm, d), lambda i, j: (i, 0))
    return pl.pallas_call(
        _mix_kernel,
        grid=(m // tm, nj),
        in_specs=[row, row, row,
                  pl.BlockSpec((d, tn), lambda i, j: (0, j)),
                  pl.BlockSpec((d, tn), lambda i, j: (0, j + nj)),
                  pl.BlockSpec((1, tn), lambda i, j: (0, j)),
                  pl.BlockSpec((1, tn), lambda i, j: (0, j + nj)),
                  pl.BlockSpec((d, tn), lambda i, j: (0, j)),
                  pl.BlockSpec((d, tn), lambda i, j: (0, j))],
        out_specs=pl.BlockSpec((tm, tn), lambda i, j: (i, j)),
        out_shape=jax.ShapeDtypeStruct((m, d), BF16),
        scratch_shapes=[pltpu.VMEM((tm, d), BF16)],
        compiler_params=_cparams(("parallel", "arbitrary")),
        name="branch_mix",
    )(x2, o_diff, o_nsa, wg, wg, bg, bg, w_br_diff.astype(BF16), w_br_nsa.astype(BF16))


def _layer_norm_rows(y, g, b):
    mu = jnp.mean(y, axis=-1, keepdims=True)
    yc = y - mu
    var = jnp.mean(yc * yc, axis=-1, keepdims=True)
    return yc * lax.rsqrt(var + LN_EPS) * g + b


def _split_bf16(a):
    hi = a.astype(BF16)
    lo = (a - hi.astype(F32)).astype(BF16)
    return hi, lo


def _post_attn_kernel(mx_ref, x_ref, wo_ref, g_ref, b_ref, wr_hi_ref, wr_lo_ref, br_ref,
                      h_ref, e_ref, gate_ref, rank_ref, cnt_ref, run_ref, *, tm, alpha):
    i = pl.program_id(0)

    @pl.when(i == 0)
    def _():
        run_ref[...] = jnp.zeros(run_ref.shape, F32)

    y = alpha * x_ref[...] + jnp.dot(mx_ref[...], wo_ref[...], preferred_element_type=F32)
    h = _layer_norm_rows(y, g_ref[...], b_ref[...])
    h_ref[...] = h
    h_hi, h_lo = _split_bf16(h)
    logits = (jnp.dot(h_hi, wr_hi_ref[...], preferred_element_type=F32)
              + jnp.dot(h_hi, wr_lo_ref[...], preferred_element_type=F32)
              + jnp.dot(h_lo, wr_hi_ref[...], preferred_element_type=F32)) + br_ref[...]
    lane = lax.broadcasted_iota(jnp.int32, (tm, N_EXPERTS), 1)
    lane4 = lax.broadcasted_iota(jnp.int32, (tm, TOP_K), 1)
    xl = logits
    vals, idxs, hits = [], [], []
    for _ in range(TOP_K):
        mx = jnp.max(xl, axis=-1, keepdims=True)
        idx = jnp.min(jnp.where(xl == mx, lane, N_EXPERTS), axis=-1, keepdims=True)
        hit = lane == idx
        vals.append(mx)
        idxs.append(idx)
        hits.append(hit)
        xl = jnp.where(hit, LOWEST, xl)
    ex = [jnp.exp(v - vals[0]) for v in vals]
    den = ex[0] + ex[1] + ex[2] + ex[3]
    member = jnp.zeros((tm, N_EXPERTS), F32)
    for hit in hits:
        member = jnp.where(hit, 1.0, member)
    r_i = lax.broadcasted_iota(jnp.int32, (tm, tm), 0)
    c_i = lax.broadcasted_iota(jnp.int32, (tm, tm), 1)
    tri = jnp.where(c_i < r_i, 1.0, 0.0).astype(BF16)
    before = jnp.dot(tri, member.astype(BF16), preferred_element_type=F32) + run_ref[...]
    e_out = jnp.zeros((tm, TOP_K), jnp.int32)
    g_out = jnp.zeros((tm, TOP_K), F32)
    r_out = jnp.zeros((tm, TOP_K), jnp.int32)
    for k in range(TOP_K):
        rk = jnp.sum(jnp.where(hits[k], before, 0.0), axis=-1, keepdims=True)
        e_out = jnp.where(lane4 == k, idxs[k], e_out)
        g_out = jnp.where(lane4 == k, ex[k] / den, g_out)
        r_out = jnp.where(lane4 == k, rk.astype(jnp.int32), r_out)
    e_ref[...] = e_out
    gate_ref[...] = g_out
    rank_ref[...] = r_out
    run_ref[...] = run_ref[...] + jnp.sum(member, axis=0, keepdims=True)
    cnt_ref[...] = run_ref[...].astype(jnp.int32)


def _post_attn(mixed, x2, w_o, ln_g, ln_b, w_router, b_router, *, alpha, tm=256):
    m, d = x2.shape
    tm = min(tm, m)
    wr_hi, wr_lo = _split_bf16(w_router.astype(F32))
    vec = lambda a: a.reshape(1, -1).astype(F32)
    full = lambda r, c: pl.BlockSpec((r, c), lambda i: (0, 0))
    rows = lambda c: pl.BlockSpec((tm, c), lambda i: (i, 0))
    return pl.pallas_call(
        functools.partial(_post_attn_kernel, tm=tm, alpha=alpha),
        grid=(m // tm,),
        in_specs=[rows(d), rows(d), full(d, d), full(1, d), full(1, d),
                  full(d, N_EXPERTS), full(d, N_EXPERTS), full(1, N_EXPERTS)],
        out_specs=[rows(d), rows(TOP_K), rows(TOP_K), rows(TOP_K), full(1, N_EXPERTS)],
        out_shape=[jax.ShapeDtypeStruct((m, d), F32),
                   jax.ShapeDtypeStruct((m, TOP_K), jnp.int32),
                   jax.ShapeDtypeStruct((m, TOP_K), F32),
                   jax.ShapeDtypeStruct((m, TOP_K), jnp.int32),
                   jax.ShapeDtypeStruct((1, N_EXPERTS), jnp.int32)],
        scratch_shapes=[pltpu.VMEM((1, N_EXPERTS), F32)],
        compiler_params=_cparams(("arbitrary",)),
        name="post_attn_router",
    )(mixed, x2, w_o.astype(BF16), vec(ln_g), vec(ln_b), wr_hi, wr_lo, vec(b_router))


def _dest_kernel(ps_ref, e_ref, r_ref, o_ref):
    e = e_ref[...]
    base = jnp.zeros(e.shape, jnp.int32)
    for j in range(N_EXPERTS):
        base = jnp.where(e == j, ps_ref[j], base)
    o_ref[...] = base + r_ref[...]


def _dest(top_e, rank, pstart, *, tm=2048):
    m = top_e.shape[0]
    tm = min(tm, m)
    grid_spec = pltpu.PrefetchScalarGridSpec(
        num_scalar_prefetch=1,
        grid=(m // tm,),
        in_specs=[pl.BlockSpec((tm, TOP_K), lambda i, ps: (i, 0)),
                  pl.BlockSpec((tm, TOP_K), lambda i, ps: (i, 0))],
        out_specs=pl.BlockSpec((tm, TOP_K), lambda i, ps: (i, 0)),
    )
    return pl.pallas_call(
        _dest_kernel, grid_spec=grid_spec,
        out_shape=jax.ShapeDtypeStruct((m, TOP_K), jnp.int32),
        compiler_params=_cparams(("parallel",)),
        name="moe_dest",
    )(pstart, top_e, rank)


DMA_UNROLL = 8


def _dispatch_kernel(dest_ref, h_ref, xs_in_ref, xs_ref, sem, *, tm):
    del xs_in_ref

    def row_copy(r, k):
        return pltpu.make_async_copy(h_ref.at[pl.ds(r, 1), :],
                                     xs_ref.at[pl.ds(dest_ref[r * TOP_K + k], 1), :], sem)

    def start(i, carry):
        for u in range(DMA_UNROLL // TOP_K):
            for k in range(TOP_K):
                row_copy(i * (DMA_UNROLL // TOP_K) + u, k).start()
        return carry

    def wait(i, carry):
        for u in range(DMA_UNROLL // TOP_K):
            for k in range(TOP_K):
                row_copy(i * (DMA_UNROLL // TOP_K) + u, k).wait()
        return carry

    lax.fori_loop(0, tm * TOP_K // DMA_UNROLL, start, 0)
    lax.fori_loop(0, tm * TOP_K // DMA_UNROLL, wait, 0)


def _dispatch(h, dest_flat, n_pad, *, tm=256):
    m, d = h.shape
    tm = min(tm, m)
    assert m % tm == 0 and (tm * TOP_K) % DMA_UNROLL == 0 and DMA_UNROLL % TOP_K == 0
    xs0 = jnp.zeros((n_pad, d), h.dtype)
    return pl.pallas_call(
        functools.partial(_dispatch_kernel, tm=tm),
        grid=(m // tm,),
        in_specs=[pl.BlockSpec((tm * TOP_K,), lambda i: (i,), memory_space=pltpu.SMEM),
                  pl.BlockSpec((tm, d), lambda i: (i, 0)),
                  pl.BlockSpec(memory_space=pl.ANY)],
        out_specs=pl.BlockSpec(memory_space=pl.ANY),
        out_shape=jax.ShapeDtypeStruct((n_pad, d), h.dtype),
        scratch_shapes=[pltpu.SemaphoreType.DMA(())],
        input_output_aliases={2: 0},
        compiler_params=_cparams(("arbitrary",)),
        name="moe_dispatch",
    )(dest_flat, h, xs0)


def _expert_kernel(be_ref, nu_ref, xs_ref, w1_ref, b1_ref, w2_ref, b2_ref, y_ref):
    i = pl.program_id(0)

    @pl.when(i < nu_ref[0])
    def _():
        xb = xs_ref[...].astype(BF16)
        hb = jnp.dot(xb, w1_ref[0], preferred_element_type=F32) + b1_ref[0]
        glu = jnp.minimum(hb[:, :D_FF], SWIGLU_LIMIT)
        lin = jnp.clip(hb[:, D_FF:], -SWIGLU_LIMIT, SWIGLU_LIMIT)
        act = glu * jax.nn.sigmoid(SWIGLU_ALPHA * glu) * (lin + 1.0)
        y_ref[...] = jnp.dot(act.astype(BF16), w2_ref[0], preferred_element_type=F32) + b2_ref[0]

    @pl.when(i >= nu_ref[0])
    def _():
        y_ref[...] = jnp.zeros(y_ref.shape, y_ref.dtype)


def _experts(xs, blk_expert, n_used, w1, b1, w2, b2):
    n_pad, d = xs.shape
    n_blocks = n_pad // MOE_BLOCK
    last = lambda i, nu: jnp.minimum(i, nu[0] - 1)
    grid_spec = pltpu.PrefetchScalarGridSpec(
        num_scalar_prefetch=2,
        grid=(n_blocks,),
        in_specs=[pl.BlockSpec((MOE_BLOCK, d), lambda i, be, nu: (last(i, nu), 0)),
                  pl.BlockSpec((1, d, 2 * D_FF), lambda i, be, nu: (be[last(i, nu)], 0, 0)),
                  pl.BlockSpec((1, 1, 2 * D_FF), lambda i, be, nu: (be[last(i, nu)], 0, 0)),
                  pl.BlockSpec((1, D_FF, d), lambda i, be, nu: (be[last(i, nu)], 0, 0)),
                  pl.BlockSpec((1, 1, d), lambda i, be, nu: (be[last(i, nu)], 0, 0))],
        out_specs=pl.BlockSpec((MOE_BLOCK, d), lambda i, be, nu: (i, 0)),
    )
    return pl.pallas_call(
        _expert_kernel, grid_spec=grid_spec,
        out_shape=jax.ShapeDtypeStruct((n_pad, d), F32),
        compiler_params=_cparams(("arbitrary",)),
        name="moe_experts",
    )(blk_expert, n_used, xs, w1.astype(BF16), b1.reshape(N_EXPERTS, 1, -1).astype(F32),
      w2.astype(BF16), b2.reshape(N_EXPERTS, 1, -1).astype(F32))


def _final_kernel(dest_ref, h_ref, gate_ref, p_ref, wpg_ref, bpg_ref, wpp_ref, g_ref, b_ref, y_ref,
                  o_ref, buf, sem, *, tm, alpha):
    def row_copy(r, k):
        return pltpu.make_async_copy(y_ref.at[pl.ds(dest_ref[r * TOP_K + k], 1), :],
                                     buf.at[k, pl.ds(r, 1), :], sem)

    rows_per_iter = DMA_UNROLL // TOP_K

    def start(i, carry):
        for u in range(rows_per_iter):
            for k in range(TOP_K):
                row_copy(i * rows_per_iter + u, k).start()
        return carry

    def wait(i, carry):
        for u in range(rows_per_iter):
            for k in range(TOP_K):
                row_copy(i * rows_per_iter + u, k).wait()
        return carry

    lax.fori_loop(0, tm // rows_per_iter, start, 0)
    h = h_ref[...]
    ple = (jax.nn.sigmoid(jnp.dot(h.astype(BF16), wpg_ref[...], preferred_element_type=F32) + bpg_ref[...])
           * jnp.dot(p_ref[...].astype(BF16), wpp_ref[...], preferred_element_type=F32))
    lax.fori_loop(0, tm // rows_per_iter, wait, 0)
    gate = gate_ref[...]
    ffn = gate[:, 0:1] * buf[0]
    for k in range(1, TOP_K):
        ffn = ffn + gate[:, k:k + 1] * buf[k]
    o_ref[...] = _layer_norm_rows(alpha * h + ffn + ple, g_ref[...], b_ref[...])


def _final(h, gate, dest_flat, y, p2, w_ple_gate, b_ple_gate, w_ple_proj, ln_g, ln_b, *, alpha, tm=256):
    m, d = h.shape
    tm = min(tm, m)
    vec = lambda a: a.reshape(1, -1).astype(F32)
    full = lambda r, c: pl.BlockSpec((r, c), lambda i: (0, 0))
    rows = lambda c: pl.BlockSpec((tm, c), lambda i: (i, 0))
    return pl.pallas_call(
        functools.partial(_final_kernel, tm=tm, alpha=alpha),
        grid=(m // tm,),
        in_specs=[pl.BlockSpec((tm * TOP_K,), lambda i: (i,), memory_space=pltpu.SMEM),
                  rows(d), rows(TOP_K), rows(PLE_DIM), full(d, d), full(1, d), full(PLE_DIM, d),
                  full(1, d), full(1, d), pl.BlockSpec(memory_space=pl.ANY)],
        out_specs=rows(d),
        out_shape=jax.ShapeDtypeStruct((m, d), F32),
        scratch_shapes=[pltpu.VMEM((TOP_K, tm, d), F32), pltpu.SemaphoreType.DMA(())],
        compiler_params=_cparams(("arbitrary",)),
        name="moe_combine_final",
    )(dest_flat, h, gate, p2, w_ple_gate.astype(BF16), vec(b_ple_gate), w_ple_proj.astype(BF16),
      vec(ln_g), vec(ln_b), y)


def _attention_block(x2, b, s, w_in, diff_p, nsa_p, lambda_init):
    t = b * s
    qscale = DIFF_HD ** -0.5 * LOG2E
    scale = np.ones((1, C_MAIN), np.float32)
    scale[0, :DIFF_QK] = qscale
    o_nq = 2 * DIFF_QK + DIFF_V
    scale[0, o_nq:o_nq + NSA_Q] = NSA_HD ** -0.5 * LOG2E
    proj = _matmul(x2, w_in[:, :C_MAIN].astype(BF16), jnp.asarray(scale),
                   jnp.zeros((1, C_MAIN), F32), out_dtype=BF16)
    wg = jnp.pad(w_in[:, C_MAIN:], ((0, 0), (0, 128 - NSA_GATES))).astype(BF16)
    gates = _matmul(x2, wg, jnp.ones((1, 128), F32), jnp.zeros((1, 128), F32),
                    act="sigmoid", out_dtype=F32)[:, :NSA_GATES]

    offs = np.cumsum((0,) + IN_SIZES[:-1])
    col = lambda i: proj[:, offs[i]:offs[i] + IN_SIZES[i]]
    o_diff = _diff_attention(proj.reshape(b, s, C_MAIN), *diff_p, lambda_init)

    nq = col(3).reshape(b, s, NSA_HEADS, NSA_HD).transpose(0, 2, 1, 3)
    grp = lambda i: col(i).reshape(b, s, NSA_KV, NSA_HD).transpose(0, 2, 1, 3)
    halves = lambda i: (col(i).reshape(b, s // CMP_STRIDE, CMP_STRIDE, NSA_KV, NSA_HD)
                        .transpose(0, 3, 1, 2, 4).reshape(b, NSA_KV, s // CMP_STRIDE, CMP_STRIDE * NSA_HD))
    pos_k, pos_v, phi_k1, phi_k2, phi_v1, phi_v2 = nsa_p
    k_cmp = _compress(halves(4), pos_k, phi_k1, phi_k2)
    v_cmp = _compress(halves(5), pos_v, phi_v1, phi_v2)
    o_cmp, selbias = _cmp_select(nq, k_cmp, v_cmp)
    o_slc = _slc_attention(nq, selbias, grp(6), grp(7))
    g4 = gates.reshape(b, s, NSA_HEADS, 3).transpose(0, 2, 1, 3)
    o_nsa = _win_combine(nq, grp(8), grp(9), o_cmp, o_slc, g4)
    o_nsa = o_nsa.transpose(0, 2, 1, 3).reshape(t, NSA_Q)
    return o_diff.reshape(t, DIFF_V), o_nsa


def _moe_block(h, top_e, gate, rank, counts, w1, b1, w2, b2):
    t = h.shape[0]
    n_blocks = (t * TOP_K + MOE_BLOCK - 1) // MOE_BLOCK + N_EXPERTS
    n_pad = n_blocks * MOE_BLOCK
    counts = counts.reshape(N_EXPERTS)
    pcounts = (counts + MOE_BLOCK - 1) // MOE_BLOCK * MOE_BLOCK
    pend = jnp.cumsum(pcounts)
    pstart = (pend - pcounts).astype(jnp.int32)
    blk_expert = jnp.minimum(
        jnp.searchsorted(pend, jnp.arange(n_blocks, dtype=jnp.int32) * MOE_BLOCK, side='right'),
        N_EXPERTS - 1).astype(jnp.int32)
    n_used = (pend[-1:] // MOE_BLOCK).astype(jnp.int32)
    dest = _dest(top_e, rank, pstart).reshape(t * TOP_K)
    xs = _dispatch(h, dest, n_pad)
    y = _experts(xs, blk_expert, n_used, w1, b1, w2, b2)
    return dest, y


def kernel(x, p, w_in, diff_lq1, diff_lk1, diff_lq2, diff_lk2, diff_subln_g, nsa_pos_k, nsa_pos_v,
           nsa_phi_k1, nsa_phi_k2, nsa_phi_v1, nsa_phi_v2, w_br_diff, w_br_nsa, w_mgate, b_mgate, w_o,
           ln1_g, ln1_b, w_router, b_router, w_e1, b_e1, w_e2, b_e2, w_ple_gate, b_ple_gate,
           w_ple_proj, ln2_g, ln2_b):
    b, s, d = x.shape
    depth = w_in.shape[0]
    alpha = (2.0 * depth) ** 0.25
    x2 = x.reshape(b * s, d)
    for i in range(depth):
        lambda_init = 0.8 - 0.6 * math.exp(-0.3 * i)
        o_diff, o_nsa = _attention_block(
            x2, b, s, w_in[i],
            (diff_lq1[i], diff_lk1[i], diff_lq2[i], diff_lk2[i], diff_subln_g[i]),
            (nsa_pos_k[i], nsa_pos_v[i], nsa_phi_k1[i], nsa_phi_k2[i], nsa_phi_v1[i], nsa_phi_v2[i]),
            lambda_init)
        mixed = _mix(x2, o_diff, o_nsa, w_mgate[i], b_mgate[i], w_br_diff[i], w_br_nsa[i])
        h, top_e, gate, rank, counts = _post_attn(mixed, x2, w_o[i], ln1_g[i], ln1_b[i],
                                                  w_router[i], b_router[i], alpha=alpha)
        dest, y = _moe_block(h, top_e, gate, rank, counts, w_e1[i], b_e1[i], w_e2[i], b_e2[i])
        x2 = _final(h, gate, dest, y, p[i].reshape(b * s, -1), w_ple_gate[i], b_ple_gate[i],
                    w_ple_proj[i], ln2_g[i], ln2_b[i], alpha=alpha)
    return x2.reshape(b, s, d)
```

```python
import functools
import math

import numpy as np
import jax
import jax.numpy as jnp
from jax import lax
from jax.experimental import pallas as pl
from jax.experimental.pallas import tpu as pltpu

D_MODEL = 1024
DIFF_HEADS = 8
DIFF_HD = 64
NSA_HEADS = 16
NSA_KV = 4
NSA_HD = 64
HPG = NSA_HEADS // NSA_KV
CMP_LEN = 32
CMP_STRIDE = 16
SLC_LEN = 64
SLC_SHIFT = 6
SLC_TOPN = 16
WIN = 512
PHI_HID = 256
N_EXPERTS = 32
TOP_K = 4
D_FF = 1024
SWIGLU_LIMIT = 7.0
SWIGLU_ALPHA = 1.702
MOE_BLOCK = 512
PLE_DIM = 256
LN_EPS = 1e-5
NEG = -1e30
BIG = 1e30
LOWEST = -3e38
LOG2E = 1.4426950408889634

DIFF_QK = DIFF_HEADS * 2 * DIFF_HD
DIFF_V = DIFF_HEADS * 2 * DIFF_HD
NSA_Q = NSA_HEADS * NSA_HD
NSA_KVW = NSA_KV * NSA_HD
NSA_GATES = NSA_HEADS * 3
IN_SIZES = (DIFF_QK, DIFF_QK, DIFF_V, NSA_Q) + (NSA_KVW,) * 6 + (NSA_GATES,)
C_MAIN = sum(IN_SIZES[:-1])

VMEM_LIMIT = 48 * 1024 * 1024

BF16 = jnp.bfloat16
F32 = jnp.float32


def _alibi_slopes_log2(n):
    return (2.0 ** (-8.0 * np.arange(1, n + 1) / n) * LOG2E).astype(np.float32)


def _cparams(sem, vmem=VMEM_LIMIT):
    return pltpu.CompilerParams(dimension_semantics=sem, vmem_limit_bytes=vmem)


PROJ_CHUNK = 512
GATE_PAD = 128


def _proj_kernel(x_ref, w_ref, s_ref, o_ref, g_ref):
    xb = x_ref[...].astype(BF16)
    for c in range(C_MAIN // PROJ_CHUNK):
        cs = slice(c * PROJ_CHUNK, (c + 1) * PROJ_CHUNK)
        acc = jnp.dot(xb, w_ref[:, cs], preferred_element_type=F32)
        o_ref[:, cs] = (acc * s_ref[:, cs]).astype(o_ref.dtype)
    gl = jnp.dot(xb, w_ref[:, C_MAIN:], preferred_element_type=F32)
    g_ref[...] = jax.nn.sigmoid(gl)


def _input_projection(x2, w_in, *, tm=512):
    m, k = x2.shape
    tm = min(tm, m)
    assert m % tm == 0 and C_MAIN % PROJ_CHUNK == 0
    w = jnp.pad(w_in, ((0, 0), (0, GATE_PAD - NSA_GATES))).astype(BF16)
    scale = np.ones((1, C_MAIN), np.float32)
    scale[0, :DIFF_QK] = DIFF_HD ** -0.5 * LOG2E
    o_nq = 2 * DIFF_QK + DIFF_V
    scale[0, o_nq:o_nq + NSA_Q] = NSA_HD ** -0.5 * LOG2E
    n_all = C_MAIN + GATE_PAD
    return pl.pallas_call(
        _proj_kernel,
        grid=(m // tm,),
        in_specs=[pl.BlockSpec((tm, k), lambda i: (i, 0)),
                  pl.BlockSpec((k, n_all), lambda i: (0, 0)),
                  pl.BlockSpec((1, C_MAIN), lambda i: (0, 0))],
        out_specs=[pl.BlockSpec((tm, C_MAIN), lambda i: (i, 0)),
                   pl.BlockSpec((tm, GATE_PAD), lambda i: (i, 0))],
        out_shape=[jax.ShapeDtypeStruct((m, C_MAIN), BF16),
                   jax.ShapeDtypeStruct((m, GATE_PAD), F32)],
        compiler_params=_cparams(("parallel",)),
        name="input_projection",
    )(x2, w, jnp.asarray(scale))


LANES = 128


def _online_update(s, v, m_old, l_old, acc_old):
    tk = s.shape[1]
    hd = acc_old.shape[1]
    m_new = jnp.maximum(m_old, jnp.max(s, axis=-1, keepdims=True))
    alpha = jnp.exp2(m_old - m_new)
    p = jnp.exp2(s - jnp.concatenate([m_new] * (tk // LANES), axis=1))
    psum = p[:, :LANES]
    for c in range(1, tk // LANES):
        psum = psum + p[:, c * LANES:(c + 1) * LANES]
    l_new = alpha * l_old + psum
    acc_new = alpha[:, :hd] * acc_old + jnp.dot(p.astype(BF16), v, preferred_element_type=F32)
    return m_new, l_new, acc_new


def _online_step(s, v, m_ref, l_ref, acc_ref):
    m_new, l_new, acc_new = _online_update(s, v, m_ref[...], l_ref[...], acc_ref[...])
    m_ref[...] = m_new
    l_ref[...] = l_new
    acc_ref[...] = acc_new


def _row_total(l):
    return jnp.sum(l, axis=-1, keepdims=True)


def _tri_steps(n):
    qi, ki = [], []
    for q in range(n):
        for k in range(q + 1):
            qi.append(q)
            ki.append(k)
    return np.asarray(qi, np.int32), np.asarray(ki, np.int32)


def _diff_kernel(qi_ref, ki_ref, sl_ref, q_ref, k_ref, v_ref,
                 lq1_ref, lk1_ref, lq2_ref, lk2_ref, g_ref, o_ref,
                 qm0, qm1, m0, l0, a0, m1, l1, a1, *, t, lambda_init):
    h = pl.program_id(1)
    step = pl.program_id(2)
    qt = qi_ref[step]
    kt = ki_ref[step]
    slope = sl_ref[h]

    @pl.when(kt == 0)
    def _():
        q = q_ref[0]
        lane = lax.broadcasted_iota(jnp.int32, q.shape, 1)
        qm0[...] = jnp.where(lane < DIFF_HD, q, jnp.zeros_like(q))
        qm1[...] = jnp.where(lane >= DIFF_HD, q, jnp.zeros_like(q))
        for m, l, a in ((m0, l0, a0), (m1, l1, a1)):
            m[...] = jnp.full(m.shape, NEG, F32)
            l[...] = jnp.zeros(l.shape, F32)
            a[...] = jnp.zeros(a.shape, F32)

    col = lax.broadcasted_iota(jnp.int32, (1, t), 1)
    bias = slope * ((kt - qt) * t + col).astype(F32)
    k = k_ref[0]
    v = v_ref[0]

    def scores(qm):
        return lax.dot_general(qm[...], k, (((1,), (1,)), ((), ())), preferred_element_type=F32)

    @pl.when(kt < qt)
    def _():
        _online_step(scores(qm0) + bias, v, m0, l0, a0)
        _online_step(scores(qm1) + bias, v, m1, l1, a1)

    @pl.when(kt == qt)
    def _():
        row = lax.broadcasted_iota(jnp.int32, (t, t), 0)
        colf = lax.broadcasted_iota(jnp.int32, (t, t), 1)
        bias_c = jnp.where(colf <= row, bias, NEG)
        _online_step(scores(qm0) + bias_c, v, m0, l0, a0)
        _online_step(scores(qm1) + bias_c, v, m1, l1, a1)
        lam = (jnp.exp(jnp.sum(lq1_ref[...] * lk1_ref[...], axis=-1, keepdims=True))
               - jnp.exp(jnp.sum(lq2_ref[...] * lk2_ref[...], axis=-1, keepdims=True))
               + lambda_init)
        o = a0[...] / _row_total(l0[...]) - lam * (a1[...] / _row_total(l1[...]))
        ms = jnp.mean(o * o, axis=-1, keepdims=True)
        o = o * lax.rsqrt(ms + LN_EPS) * g_ref[...] * (1.0 - lambda_init)
        o_ref[0] = o.astype(o_ref.dtype)


def _diff_attention(proj3, lq1, lk1, lq2, lk2, g, lambda_init, *, t=1024):
    b, s, _ = proj3.shape
    t = min(t, s)
    qi, ki = _tri_steps(s // t)
    w = 2 * DIFF_HD
    slopes = jnp.asarray(_alibi_slopes_log2(DIFF_HEADS))
    vec = lambda a: a.reshape(1, -1).astype(F32)
    small = lambda c: pl.BlockSpec((1, c), lambda b_, h, st, qi, ki, sl: (0, 0))
    grid_spec = pltpu.PrefetchScalarGridSpec(
        num_scalar_prefetch=3,
        grid=(b, DIFF_HEADS, len(qi)),
        in_specs=[pl.BlockSpec((1, t, w), lambda b_, h, st, qi, ki, sl: (b_, qi[st], h)),
                  pl.BlockSpec((1, t, w), lambda b_, h, st, qi, ki, sl: (b_, ki[st], DIFF_HEADS + h)),
                  pl.BlockSpec((1, t, w), lambda b_, h, st, qi, ki, sl: (b_, ki[st], 2 * DIFF_HEADS + h)),
                  small(DIFF_HD), small(DIFF_HD), small(DIFF_HD), small(DIFF_HD), small(w)],
        out_specs=pl.BlockSpec((1, t, w), lambda b_, h, st, qi, ki, sl: (b_, qi[st], h)),
        scratch_shapes=[pltpu.VMEM((t, w), BF16), pltpu.VMEM((t, w), BF16),
                        pltpu.VMEM((t, LANES), F32), pltpu.VMEM((t, LANES), F32), pltpu.VMEM((t, w), F32),
                        pltpu.VMEM((t, LANES), F32), pltpu.VMEM((t, LANES), F32), pltpu.VMEM((t, w), F32)],
    )
    return pl.pallas_call(
        functools.partial(_diff_kernel, t=t, lambda_init=lambda_init),
        grid_spec=grid_spec,
        out_shape=jax.ShapeDtypeStruct((b, s, DIFF_V), BF16),
        compiler_params=_cparams(("parallel", "parallel", "arbitrary")),
        name="diff_attention",
    )(jnp.asarray(qi), jnp.asarray(ki), slopes, proj3, proj3, proj3,
      vec(lq1), vec(lk1), vec(lq2), vec(lk2), vec(g))


def _compress_kernel(a_ref, w1_ref, w2_ref, pos_ref, o_ref):
    a = a_ref[0, 0]
    half = CMP_STRIDE * NSA_HD
    top = jnp.dot(a, w1_ref[:half, :], preferred_element_type=F32)
    bot = jnp.dot(a, w1_ref[half:, :], preferred_element_type=F32)
    pos = pos_ref[...]
    pos_hi = pos.astype(BF16)
    pos_lo = (pos - pos_hi.astype(F32)).astype(BF16)
    c = (jnp.dot(pos_hi, w1_ref[...], preferred_element_type=F32)
         + jnp.dot(pos_lo, w1_ref[...], preferred_element_type=F32))[0:1]
    n = a.shape[0]
    hid = top + pltpu.roll(bot, n - 1, axis=0) + c
    hid = hid * jax.nn.sigmoid(hid)
    o_ref[0, 0] = jnp.dot(hid.astype(BF16), w2_ref[...], preferred_element_type=F32).astype(o_ref.dtype)


def _compress(t_halves, pos, w1, w2):
    b, g, nh, _ = t_halves.shape
    pos8 = jnp.broadcast_to(pos.reshape(1, -1).astype(F32), (8, CMP_LEN * NSA_HD))
    return pl.pallas_call(
        _compress_kernel,
        grid=(b, g),
        in_specs=[pl.BlockSpec((1, 1, nh, CMP_STRIDE * NSA_HD), lambda i, j: (i, j, 0, 0)),
                  pl.BlockSpec((CMP_LEN * NSA_HD, PHI_HID), lambda i, j: (0, 0)),
                  pl.BlockSpec((PHI_HID, NSA_HD), lambda i, j: (0, 0)),
                  pl.BlockSpec((8, CMP_LEN * NSA_HD), lambda i, j: (0, 0))],
        out_specs=pl.BlockSpec((1, 1, nh, NSA_HD), lambda i, j: (i, j, 0, 0)),
        out_shape=jax.ShapeDtypeStruct((b, g, nh, NSA_HD), BF16),
        compiler_params=_cparams(("parallel", "parallel")),
        name="nsa_compress",
    )(t_halves, w1.astype(BF16), w2.astype(BF16), pos8)


SEL_LANES = 64


def _cmp_select_kernel(sl_ref, q_ref, kc_ref, vc_ref, o_ref, qs_ref, *, tq, n_cmp_pad, n_slc, n_top):
    g = pl.program_id(1)
    qt = pl.program_id(2)
    t_col = qt * tq + lax.broadcasted_iota(jnp.int32, (tq, 1), 0)
    cmp_pos = lax.broadcasted_iota(jnp.int32, (1, n_cmp_pad), 1) * CMP_STRIDE + (CMP_LEN - 1)
    mask_add = jnp.where(t_col >= cmp_pos, 0.0, NEG)
    row_ok = jnp.where(t_col >= CMP_LEN - 1, 1.0, 0.0)
    cmp_posf = (cmp_pos - qt * tq).astype(F32)
    kc = kc_ref[0, 0]
    vc = vc_ref[0, 0]
    psum = jnp.zeros((tq, n_cmp_pad), F32)
    for hh in range(HPG):
        slope = sl_ref[g * HPG + hh]
        s = lax.dot_general(q_ref[0, hh], kc, (((1,), (1,)), ((), ())), preferred_element_type=F32)
        s = s + (mask_add + slope * cmp_posf)
        m = jnp.max(s, axis=-1, keepdims=True)
        p = jnp.exp2(s - m)
        l = jnp.sum(p, axis=-1, keepdims=True)
        p = p * (row_ok / l)
        o_ref[0, hh] = jnp.dot(p.astype(BF16), vc, preferred_element_type=F32)
        psum = psum + p
    c_lo = lax.broadcasted_iota(jnp.int32, (n_cmp_pad, 128), 0) * CMP_STRIDE
    s_lo = lax.broadcasted_iota(jnp.int32, (n_cmp_pad, 128), 1) * SLC_LEN
    ov = jnp.minimum(c_lo + CMP_LEN, s_lo + SLC_LEN) - jnp.maximum(c_lo, s_lo)
    cmap = (jnp.maximum(ov, 0).astype(F32) * (1.0 / CMP_LEN)).astype(BF16)
    ps_hi = psum.astype(BF16)
    ps_lo = (psum - ps_hi.astype(F32)).astype(BF16)
    imp = (jnp.dot(ps_hi, cmap, preferred_element_type=F32)
           + jnp.dot(ps_lo, cmap, preferred_element_type=F32))
    imp_t = jnp.transpose(imp)[:n_slc]
    blk = lax.broadcasted_iota(jnp.int32, (n_slc, tq), 0)
    t_row = qt * tq + lax.broadcasted_iota(jnp.int32, (n_slc, tq), 1)
    cur = lax.shift_right_logical(t_row, SLC_SHIFT)
    forced = (blk == 0) | (blk == cur) | (blk == cur - 1)
    x = jnp.where(forced, BIG, jnp.where(blk <= cur, imp_t, NEG))
    sel = jnp.zeros((n_slc, tq), F32)
    for _ in range(n_top):
        mx = jnp.max(x, axis=0, keepdims=True)
        idx = jnp.min(jnp.where(x == mx, blk, n_slc), axis=0, keepdims=True)
        hit = blk == idx
        sel = jnp.where(hit, 1.0, sel)
        x = jnp.where(hit, LOWEST, x)
    if n_slc < 128:
        sel = jnp.concatenate([sel, jnp.zeros((128 - n_slc, tq), F32)], axis=0)
    sel_q = jnp.transpose(sel)[:, :SEL_LANES]
    sel_bias = jnp.where(sel_q > 0.5, 0.0, NEG)
    for hh in range(HPG):
        qs_ref[0, hh] = jnp.concatenate([q_ref[0, hh].astype(F32), sel_bias], axis=1).astype(qs_ref.dtype)


def _cmp_select(q, k_cmp, v_cmp, *, tq=512):
    b, _, s, _ = q.shape
    tq = min(tq, s)
    n_cmp_pad = k_cmp.shape[2]
    n_slc = s // SLC_LEN
    n_top = min(SLC_TOPN, n_slc)
    assert n_slc <= SEL_LANES
    slopes = jnp.asarray(_alibi_slopes_log2(NSA_HEADS))
    grid_spec = pltpu.PrefetchScalarGridSpec(
        num_scalar_prefetch=1,
        grid=(b, NSA_KV, s // tq),
        in_specs=[pl.BlockSpec((1, HPG, tq, NSA_HD), lambda i, j, k, sl: (i, j, k, 0)),
                  pl.BlockSpec((1, 1, n_cmp_pad, NSA_HD), lambda i, j, k, sl: (i, j, 0, 0)),
                  pl.BlockSpec((1, 1, n_cmp_pad, NSA_HD), lambda i, j, k, sl: (i, j, 0, 0))],
        out_specs=[pl.BlockSpec((1, HPG, tq, NSA_HD), lambda i, j, k, sl: (i, j, k, 0)),
                   pl.BlockSpec((1, HPG, tq, NSA_HD + SEL_LANES), lambda i, j, k, sl: (i, j, k, 0))],
    )
    return pl.pallas_call(
        functools.partial(_cmp_select_kernel, tq=tq, n_cmp_pad=n_cmp_pad, n_slc=n_slc, n_top=n_top),
        grid_spec=grid_spec,
        out_shape=[jax.ShapeDtypeStruct((b, NSA_HEADS, s, NSA_HD), F32),
                   jax.ShapeDtypeStruct((b, NSA_HEADS, s, NSA_HD + SEL_LANES), BF16)],
        compiler_params=_cparams(("parallel", "parallel", "parallel")),
        name="nsa_cmp_select",
    )(slopes, q, k_cmp, v_cmp)


def _slc_kernel(qi_ref, ki_ref, sl_ref, q_ref, k_ref, v_ref, o_ref, m_sc, l_sc, a_sc, *, t):
    g = pl.program_id(1)
    step = pl.program_id(2)
    qt = qi_ref[step]
    kt = ki_ref[step]

    @pl.when(kt == 0)
    def _():
        m_sc[...] = jnp.full(m_sc.shape, NEG, F32)
        l_sc[...] = jnp.zeros(l_sc.shape, F32)
        a_sc[...] = jnp.zeros(a_sc.shape, F32)

    col = lax.broadcasted_iota(jnp.int32, (1, t), 1)
    rel = ((kt - qt) * t + col).astype(F32)
    k = k_ref[0, 0]
    v = v_ref[0, 0]

    def sweep(mask_add):
        for hh in range(HPG):
            s = lax.dot_general(q_ref[0, hh], k, (((1,), (1,)), ((), ())), preferred_element_type=F32)
            bias = sl_ref[g * HPG + hh] * rel
            s = s + (bias if mask_add is None else mask_add + bias)
            m_new, l_new, a_new = _online_update(s, v, m_sc[hh], l_sc[hh], a_sc[hh])
            m_sc[hh] = m_new
            l_sc[hh] = l_new
            a_sc[hh] = a_new

    @pl.when(kt < qt)
    def _():
        sweep(None)

    @pl.when(kt == qt)
    def _():
        row = lax.broadcasted_iota(jnp.int32, (t, t), 0)
        colf = lax.broadcasted_iota(jnp.int32, (t, t), 1)
        sweep(jnp.where(colf <= row, 0.0, NEG))
        for hh in range(HPG):
            o_ref[0, hh] = a_sc[hh] / _row_total(l_sc[hh])


def _slc_attention(qs, kx, vs, *, t=1024):
    b, _, s, w = qs.shape
    t = min(t, s)
    qi, ki = _tri_steps(s // t)
    slopes = jnp.asarray(_alibi_slopes_log2(NSA_HEADS))
    grid_spec = pltpu.PrefetchScalarGridSpec(
        num_scalar_prefetch=3,
        grid=(b, NSA_KV, len(qi)),
        in_specs=[pl.BlockSpec((1, HPG, t, w), lambda i, g, st, qi, ki, sl: (i, g, qi[st], 0)),
                  pl.BlockSpec((1, 1, t, w), lambda i, g, st, qi, ki, sl: (i, g, ki[st], 0)),
                  pl.BlockSpec((1, 1, t, NSA_HD), lambda i, g, st, qi, ki, sl: (i, g, ki[st], 0))],
        out_specs=pl.BlockSpec((1, HPG, t, NSA_HD), lambda i, g, st, qi, ki, sl: (i, g, qi[st], 0)),
        scratch_shapes=[pltpu.VMEM((HPG, t, LANES), F32), pltpu.VMEM((HPG, t, LANES), F32),
                        pltpu.VMEM((HPG, t, NSA_HD), F32)],
    )
    return pl.pallas_call(
        functools.partial(_slc_kernel, t=t),
        grid_spec=grid_spec,
        out_shape=jax.ShapeDtypeStruct((b, NSA_HEADS, s, NSA_HD), F32),
        compiler_params=_cparams(("parallel", "parallel", "arbitrary")),
        name="nsa_slc_attention",
    )(jnp.asarray(qi), jnp.asarray(ki), slopes, qs, kx, vs)


def _win_kernel(sl_ref, q_ref, *rest, t):
    nblk = WIN // t
    k_refs = rest[:nblk + 1]
    v_refs = rest[nblk + 1:2 * nblk + 2]
    oc_ref, os_ref, g_ref, o_ref = rest[2 * nblk + 2:]
    g = pl.program_id(1)
    qt = pl.program_id(2)
    row = lax.broadcasted_iota(jnp.int32, (t, t), 0)
    col = lax.broadcasted_iota(jnp.int32, (t, t), 1)
    colr = lax.broadcasted_iota(jnp.int32, (1, t), 1)
    refs = tuple(zip(k_refs, v_refs))
    masks = []
    for j in range(len(refs)):
        off = j - nblk
        if off == -nblk:
            masks.append(jnp.where((col > row) & (qt + off >= 0), 0.0, NEG))
        elif off < 0:
            masks.append(jnp.where(qt + off >= 0, 0.0, NEG))
        else:
            masks.append(jnp.where(col <= row, 0.0, NEG))
    for hh in range(HPG):
        slope = sl_ref[g * HPG + hh]
        q = q_ref[0, hh]
        m = jnp.full((t, LANES), NEG, F32)
        l = jnp.zeros((t, LANES), F32)
        acc = jnp.zeros((t, NSA_HD), F32)
        for j, (k_ref, v_ref) in enumerate(refs):
            off = j - nblk
            s = lax.dot_general(q, k_ref[0, 0], (((1,), (1,)), ((), ())), preferred_element_type=F32)
            s = s + (masks[j] + slope * (off * t + colr).astype(F32))
            m, l, acc = _online_update(s, v_ref[0, 0], m, l, acc)
        o_win = acc / _row_total(l)
        gt = g_ref[0, hh]
        out = gt[:, 0:1] * oc_ref[0, hh] + gt[:, 1:2] * os_ref[0, hh] + gt[:, 2:3] * o_win
        o_ref[0, hh] = out.astype(o_ref.dtype)


def _win_combine(q, kw, vw, o_cmp, o_slc, gates, *, t=512):
    b, _, s, _ = q.shape
    t = min(t, s)
    assert WIN % t == 0
    nblk = WIN // t
    slopes = jnp.asarray(_alibi_slopes_log2(NSA_HEADS))
    kv = lambda off: pl.BlockSpec((1, 1, t, NSA_HD),
                                  lambda i, g, qt, sl: (i, g, jnp.maximum(qt + off, 0), 0))
    per_head = lambda w: pl.BlockSpec((1, HPG, t, w), lambda i, g, qt, sl: (i, g, qt, 0))
    grid_spec = pltpu.PrefetchScalarGridSpec(
        num_scalar_prefetch=1,
        grid=(b, NSA_KV, s // t),
        in_specs=([per_head(NSA_HD)] + [kv(off) for off in range(-nblk, 1)] * 2
                  + [per_head(NSA_HD), per_head(NSA_HD), per_head(3)]),
        out_specs=per_head(NSA_HD),
    )
    return pl.pallas_call(
        functools.partial(_win_kernel, t=t),
        grid_spec=grid_spec,
        out_shape=jax.ShapeDtypeStruct((b, NSA_HEADS, s, NSA_HD), BF16),
        compiler_params=_cparams(("parallel", "parallel", "parallel")),
        name="nsa_win_combine",
    )(slopes, q, *([kw] * (nblk + 1)), *([vw] * (nblk + 1)), o_cmp, o_slc, gates)


def _mix_kernel(x_ref, od_ref, on_ref, wg1_ref, wg2_ref, bg1_ref, bg2_ref, wd_ref, wn_ref, o_ref, xb_ref):
    @pl.when(pl.program_id(1) == 0)
    def _():
        xb_ref[...] = x_ref[...].astype(BF16)

    xb = xb_ref[...]
    g1 = jax.nn.sigmoid(jnp.dot(xb, wg1_ref[...], preferred_element_type=F32) + bg1_ref[...])
    g2 = jax.nn.sigmoid(jnp.dot(xb, wg2_ref[...], preferred_element_type=F32) + bg2_ref[...])
    bd = jnp.dot(od_ref[...], wd_ref[...], preferred_element_type=F32)
    bn = jnp.dot(on_ref[...], wn_ref[...], preferred_element_type=F32)
    o_ref[...] = (g1 * bd + g2 * bn).astype(o_ref.dtype)


def _mix(x2, o_diff, o_nsa, w_mgate, b_mgate, w_br_diff, w_br_nsa, *, tm=512, tn=512):
    m, d = x2.shape
    tm = min(tm, m)
    nj = d // tn
    wg = w_mgate.astype(BF16)
    bg = b_mgate.reshape(1, -1).astype(F32)
    row = pl.BlockSpec((tm, d), lambda i, j: (i, 0))
    return pl.pallas_call(
        _mix_kernel,
        grid=(m // tm, nj),
        in_specs=[row, row, row,
                  pl.BlockSpec((d, tn), lambda i, j: (0, j)),
                  pl.BlockSpec((d, tn), lambda i, j: (0, j + nj)),
                  pl.BlockSpec((1, tn), lambda i, j: (0, j)),
                  pl.BlockSpec((1, tn), lambda i, j: (0, j + nj)),
                  pl.BlockSpec((d, tn), lambda i, j: (0, j)),
                  pl.BlockSpec((d, tn), lambda i, j: (0, j))],
        out_specs=pl.BlockSpec((tm, tn), lambda i, j: (i, j)),
        out_shape=jax.ShapeDtypeStruct((m, d), BF16),
        scratch_shapes=[pltpu.VMEM((tm, d), BF16)],
        compiler_params=_cparams(("parallel", "arbitrary")),
        name="branch_mix",
    )(x2, o_diff, o_nsa, wg, wg, bg, bg, w_br_diff.astype(BF16), w_br_nsa.astype(BF16))


def _layer_norm_rows(y, g, b):
    mu = jnp.mean(y, axis=-1, keepdims=True)
    yc = y - mu
    var = jnp.mean(yc * yc, axis=-1, keepdims=True)
    return yc * lax.rsqrt(var + LN_EPS) * g + b


def _split_bf16(a):
    hi = a.astype(BF16)
    lo = (a - hi.astype(F32)).astype(BF16)
    return hi, lo


def _post_attn_kernel(mx_ref, x_ref, wo_ref, g_ref, b_ref, wr_hi_ref, wr_lo_ref, br_ref,
                      h_ref, e_ref, gate_ref, rank_ref, cnt_ref, run_ref, *, tm, alpha):
    i = pl.program_id(0)

    @pl.when(i == 0)
    def _():
        run_ref[...] = jnp.zeros(run_ref.shape, F32)

    y = alpha * x_ref[...] + jnp.dot(mx_ref[...], wo_ref[...], preferred_element_type=F32)
    h = _layer_norm_rows(y, g_ref[...], b_ref[...])
    h_ref[...] = h
    h_hi, h_lo = _split_bf16(h)
    logits = (jnp.dot(h_hi, wr_hi_ref[...], preferred_element_type=F32)
              + jnp.dot(h_hi, wr_lo_ref[...], preferred_element_type=F32)
              + jnp.dot(h_lo, wr_hi_ref[...], preferred_element_type=F32)) + br_ref[...]
    lane = lax.broadcasted_iota(jnp.int32, (tm, N_EXPERTS), 1)
    lane4 = lax.broadcasted_iota(jnp.int32, (tm, TOP_K), 1)
    xl = logits
    vals, idxs, hits = [], [], []
    for _ in range(TOP_K):
        mx = jnp.max(xl, axis=-1, keepdims=True)
        idx = jnp.min(jnp.where(xl == mx, lane, N_EXPERTS), axis=-1, keepdims=True)
        hit = lane == idx
        vals.append(mx)
        idxs.append(idx)
        hits.append(hit)
        xl = jnp.where(hit, LOWEST, xl)
    ex = [jnp.exp(v - vals[0]) for v in vals]
    den = ex[0] + ex[1] + ex[2] + ex[3]
    member = jnp.zeros((tm, N_EXPERTS), F32)
    for hit in hits:
        member = jnp.where(hit, 1.0, member)
    r_i = lax.broadcasted_iota(jnp.int32, (tm, tm), 0)
    c_i = lax.broadcasted_iota(jnp.int32, (tm, tm), 1)
    tri = jnp.where(c_i < r_i, 1.0, 0.0).astype(BF16)
    before = jnp.dot(tri, member.astype(BF16), preferred_element_type=F32) + run_ref[...]
    e_out = jnp.zeros((tm, TOP_K), jnp.int32)
    g_out = jnp.zeros((tm, TOP_K), F32)
    r_out = jnp.zeros((tm, TOP_K), jnp.int32)
    for k in range(TOP_K):
        rk = jnp.sum(jnp.where(hits[k], before, 0.0), axis=-1, keepdims=True)
        e_out = jnp.where(lane4 == k, idxs[k], e_out)
        g_out = jnp.where(lane4 == k, ex[k] / den, g_out)
        r_out = jnp.where(lane4 == k, rk.astype(jnp.int32), r_out)
    e_ref[...] = e_out
    gate_ref[...] = g_out
    rank_ref[...] = r_out
    run_ref[...] = run_ref[...] + jnp.sum(member, axis=0, keepdims=True)
    cnt_ref[...] = run_ref[...].astype(jnp.int32)


def _post_attn(mixed, x2, w_o, ln_g, ln_b, w_router, b_router, *, alpha, tm=256):
    m, d = x2.shape
    tm = min(tm, m)
    wr_hi, wr_lo = _split_bf16(w_router.astype(F32))
    vec = lambda a: a.reshape(1, -1).astype(F32)
    full = lambda r, c: pl.BlockSpec((r, c), lambda i: (0, 0))
    rows = lambda c: pl.BlockSpec((tm, c), lambda i: (i, 0))
    return pl.pallas_call(
        functools.partial(_post_attn_kernel, tm=tm, alpha=alpha),
        grid=(m // tm,),
        in_specs=[rows(d), rows(d), full(d, d), full(1, d), full(1, d),
                  full(d, N_EXPERTS), full(d, N_EXPERTS), full(1, N_EXPERTS)],
        out_specs=[rows(d), rows(TOP_K), rows(TOP_K), rows(TOP_K), full(1, N_EXPERTS)],
        out_shape=[jax.ShapeDtypeStruct((m, d), F32),
                   jax.ShapeDtypeStruct((m, TOP_K), jnp.int32),
                   jax.ShapeDtypeStruct((m, TOP_K), F32),
                   jax.ShapeDtypeStruct((m, TOP_K), jnp.int32),
                   jax.ShapeDtypeStruct((1, N_EXPERTS), jnp.int32)],
        scratch_shapes=[pltpu.VMEM((1, N_EXPERTS), F32)],
        compiler_params=_cparams(("arbitrary",)),
        name="post_attn_router",
    )(mixed, x2, w_o.astype(BF16), vec(ln_g), vec(ln_b), wr_hi, wr_lo, vec(b_router))


def _dest_kernel(ps_ref, e_ref, r_ref, o_ref):
    e = e_ref[...]
    base = jnp.zeros(e.shape, jnp.int32)
    for j in range(N_EXPERTS):
        base = jnp.where(e == j, ps_ref[j], base)
    o_ref[...] = base + r_ref[...]


def _dest(top_e, rank, pstart, *, tm=2048):
    m = top_e.shape[0]
    tm = min(tm, m)
    grid_spec = pltpu.PrefetchScalarGridSpec(
        num_scalar_prefetch=1,
        grid=(m // tm,),
        in_specs=[pl.BlockSpec((tm, TOP_K), lambda i, ps: (i, 0)),
                  pl.BlockSpec((tm, TOP_K), lambda i, ps: (i, 0))],
        out_specs=pl.BlockSpec((tm, TOP_K), lambda i, ps: (i, 0)),
    )
    return pl.pallas_call(
        _dest_kernel, grid_spec=grid_spec,
        out_shape=jax.ShapeDtypeStruct((m, TOP_K), jnp.int32),
        compiler_params=_cparams(("parallel",)),
        name="moe_dest",
    )(pstart, top_e, rank)


DMA_UNROLL = 8


def _dispatch_kernel(dest_ref, h_ref, xs_in_ref, xs_ref, sem, *, tm):
    del xs_in_ref

    def row_copy(r, k):
        return pltpu.make_async_copy(h_ref.at[pl.ds(r, 1), :],
                                     xs_ref.at[pl.ds(dest_ref[r * TOP_K + k], 1), :], sem)

    def start(i, carry):
        for u in range(DMA_UNROLL // TOP_K):
            for k in range(TOP_K):
                row_copy(i * (DMA_UNROLL // TOP_K) + u, k).start()
        return carry

    def wait(i, carry):
        for u in range(DMA_UNROLL // TOP_K):
            for k in range(TOP_K):
                row_copy(i * (DMA_UNROLL // TOP_K) + u, k).wait()
        return carry

    lax.fori_loop(0, tm * TOP_K // DMA_UNROLL, start, 0)
    lax.fori_loop(0, tm * TOP_K // DMA_UNROLL, wait, 0)


def _dispatch(h, dest_flat, n_pad, *, tm=256):
    m, d = h.shape
    tm = min(tm, m)
    assert m % tm == 0 and (tm * TOP_K) % DMA_UNROLL == 0 and DMA_UNROLL % TOP_K == 0
    xs0 = jnp.zeros((n_pad, d), h.dtype)
    return pl.pallas_call(
        functools.partial(_dispatch_kernel, tm=tm),
        grid=(m // tm,),
        in_specs=[pl.BlockSpec((tm * TOP_K,), lambda i: (i,), memory_space=pltpu.SMEM),
                  pl.BlockSpec((tm, d), lambda i: (i, 0)),
                  pl.BlockSpec(memory_space=pl.ANY)],
        out_specs=pl.BlockSpec(memory_space=pl.ANY),
        out_shape=jax.ShapeDtypeStruct((n_pad, d), h.dtype),
        scratch_shapes=[pltpu.SemaphoreType.DMA(())],
        input_output_aliases={2: 0},
        compiler_params=_cparams(("arbitrary",)),
        name="moe_dispatch",
    )(dest_flat, h, xs0)


def _expert_kernel(be_ref, nu_ref, xs_ref, w1_ref, b1_ref, w2_ref, b2_ref, y_ref):
    i = pl.program_id(0)

    @pl.when(i < nu_ref[0])
    def _():
        xb = xs_ref[...].astype(BF16)
        hb = jnp.dot(xb, w1_ref[0], preferred_element_type=F32) + b1_ref[0]
        glu = jnp.minimum(hb[:, :D_FF], SWIGLU_LIMIT)
        lin = jnp.clip(hb[:, D_FF:], -SWIGLU_LIMIT, SWIGLU_LIMIT)
        act = glu * jax.nn.sigmoid(SWIGLU_ALPHA * glu) * (lin + 1.0)
        y_ref[...] = jnp.dot(act.astype(BF16), w2_ref[0], preferred_element_type=F32) + b2_ref[0]

    @pl.when(i >= nu_ref[0])
    def _():
        y_ref[...] = jnp.zeros(y_ref.shape, y_ref.dtype)


def _experts(xs, blk_expert, n_used, w1, b1, w2, b2):
    n_pad, d = xs.shape
    n_blocks = n_pad // MOE_BLOCK
    last = lambda i, nu: jnp.minimum(i, nu[0] - 1)
    grid_spec = pltpu.PrefetchScalarGridSpec(
        num_scalar_prefetch=2,
        grid=(n_blocks,),
        in_specs=[pl.BlockSpec((MOE_BLOCK, d), lambda i, be, nu: (last(i, nu), 0)),
                  pl.BlockSpec((1, d, 2 * D_FF), lambda i, be, nu: (be[last(i, nu)], 0, 0)),
                  pl.BlockSpec((1, 1, 2 * D_FF), lambda i, be, nu: (be[last(i, nu)], 0, 0)),
                  pl.BlockSpec((1, D_FF, d), lambda i, be, nu: (be[last(i, nu)], 0, 0)),
                  pl.BlockSpec((1, 1, d), lambda i, be, nu: (be[last(i, nu)], 0, 0))],
        out_specs=pl.BlockSpec((MOE_BLOCK, d), lambda i, be, nu: (i, 0)),
    )
    return pl.pallas_call(
        _expert_kernel, grid_spec=grid_spec,
        out_shape=jax.ShapeDtypeStruct((n_pad, d), F32),
        compiler_params=_cparams(("arbitrary",)),
        name="moe_experts",
    )(blk_expert, n_used, xs, w1.astype(BF16), b1.reshape(N_EXPERTS, 1, -1).astype(F32),
      w2.astype(BF16), b2.reshape(N_EXPERTS, 1, -1).astype(F32))


def _final_kernel(dest_ref, h_ref, gate_ref, p_ref, wpg_ref, bpg_ref, wpp_ref, g_ref, b_ref, y_ref,
                  o_ref, buf, sem, *, tm, alpha):
    def row_copy(r, k):
        return pltpu.make_async_copy(y_ref.at[pl.ds(dest_ref[r * TOP_K + k], 1), :],
                                     buf.at[k, pl.ds(r, 1), :], sem)

    rows_per_iter = DMA_UNROLL // TOP_K

    def start(i, carry):
        for u in range(rows_per_iter):
            for k in range(TOP_K):
                row_copy(i * rows_per_iter + u, k).start()
        return carry

    def wait(i, carry):
        for u in range(rows_per_iter):
            for k in range(TOP_K):
                row_copy(i * rows_per_iter + u, k).wait()
        return carry

    lax.fori_loop(0, tm // rows_per_iter, start, 0)
    h = h_ref[...]
    ple = (jax.nn.sigmoid(jnp.dot(h.astype(BF16), wpg_ref[...], preferred_element_type=F32) + bpg_ref[...])
           * jnp.dot(p_ref[...].astype(BF16), wpp_ref[...], preferred_element_type=F32))
    lax.fori_loop(0, tm // rows_per_iter, wait, 0)
    gate = gate_ref[...]
    ffn = gate[:, 0:1] * buf[0]
    for k in range(1, TOP_K):
        ffn = ffn + gate[:, k:k + 1] * buf[k]
    o_ref[...] = _layer_norm_rows(alpha * h + ffn + ple, g_ref[...], b_ref[...])


def _final(h, gate, dest_flat, y, p2, w_ple_gate, b_ple_gate, w_ple_proj, ln_g, ln_b, *, alpha, tm=256):
    m, d = h.shape
    tm = min(tm, m)
    vec = lambda a: a.reshape(1, -1).astype(F32)
    full = lambda r, c: pl.BlockSpec((r, c), lambda i: (0, 0))
    rows = lambda c: pl.BlockSpec((tm, c), lambda i: (i, 0))
    return pl.pallas_call(
        functools.partial(_final_kernel, tm=tm, alpha=alpha),
        grid=(m // tm,),
        in_specs=[pl.BlockSpec((tm * TOP_K,), lambda i: (i,), memory_space=pltpu.SMEM),
                  rows(d), rows(TOP_K), rows(PLE_DIM), full(d, d), full(1, d), full(PLE_DIM, d),
                  full(1, d), full(1, d), pl.BlockSpec(memory_space=pl.ANY)],
        out_specs=rows(d),
        out_shape=jax.ShapeDtypeStruct((m, d), F32),
        scratch_shapes=[pltpu.VMEM((TOP_K, tm, d), F32), pltpu.SemaphoreType.DMA(())],
        compiler_params=_cparams(("arbitrary",)),
        name="moe_combine_final",
    )(dest_flat, h, gate, p2, w_ple_gate.astype(BF16), vec(b_ple_gate), w_ple_proj.astype(BF16),
      vec(ln_g), vec(ln_b), y)


def _attention_block(x2, b, s, w_in, diff_p, nsa_p, lambda_init):
    t = b * s
    proj, gates = _input_projection(x2, w_in)
    gates = gates[:, :NSA_GATES]

    offs = np.cumsum((0,) + IN_SIZES[:-1])
    col = lambda i: proj[:, offs[i]:offs[i] + IN_SIZES[i]]
    o_diff = _diff_attention(proj.reshape(b, s, C_MAIN), *diff_p, lambda_init)

    nq = col(3).reshape(b, s, NSA_HEADS, NSA_HD).transpose(0, 2, 1, 3)
    grp = lambda i: col(i).reshape(b, s, NSA_KV, NSA_HD).transpose(0, 2, 1, 3)
    halves = lambda i: (col(i).reshape(b, s // CMP_STRIDE, CMP_STRIDE, NSA_KV, NSA_HD)
                        .transpose(0, 3, 1, 2, 4).reshape(b, NSA_KV, s // CMP_STRIDE, CMP_STRIDE * NSA_HD))
    pos_k, pos_v, phi_k1, phi_k2, phi_v1, phi_v2 = nsa_p
    k_cmp = _compress(halves(4), pos_k, phi_k1, phi_k2)
    v_cmp = _compress(halves(5), pos_v, phi_v1, phi_v2)
    o_cmp, qs = _cmp_select(nq, k_cmp, v_cmp)
    blk_onehot = (np.arange(s)[:, None] // SLC_LEN == np.arange(SEL_LANES)[None, :])
    kx = jnp.concatenate(
        [grp(6), jnp.broadcast_to(jnp.asarray(blk_onehot, BF16), (b, NSA_KV, s, SEL_LANES))], axis=-1)
    o_slc = _slc_attention(qs, kx, grp(7))
    g4 = gates.reshape(b, s, NSA_HEADS, 3).transpose(0, 2, 1, 3)
    o_nsa = _win_combine(nq, grp(8), grp(9), o_cmp, o_slc, g4)
    o_nsa = o_nsa.transpose(0, 2, 1, 3).reshape(t, NSA_Q)
    return o_diff.reshape(t, DIFF_V), o_nsa


def _moe_block(h, top_e, gate, rank, counts, w1, b1, w2, b2):
    t = h.shape[0]
    n_blocks = (t * TOP_K + MOE_BLOCK - 1) // MOE_BLOCK + N_EXPERTS
    n_pad = n_blocks * MOE_BLOCK
    counts = counts.reshape(N_EXPERTS)
    pcounts = (counts + MOE_BLOCK - 1) // MOE_BLOCK * MOE_BLOCK
    pend = jnp.cumsum(pcounts)
    pstart = (pend - pcounts).astype(jnp.int32)
    blk_start = jnp.arange(n_blocks, dtype=jnp.int32) * MOE_BLOCK
    blk_expert = jnp.minimum(jnp.sum(pend[None, :] <= blk_start[:, None], axis=1),
                             N_EXPERTS - 1).astype(jnp.int32)
    n_used = (pend[-1:] // MOE_BLOCK).astype(jnp.int32)
    dest = _dest(top_e, rank, pstart).reshape(t * TOP_K)
    xs = _dispatch(h, dest, n_pad)
    y = _experts(xs, blk_expert, n_used, w1, b1, w2, b2)
    return dest, y


def kernel(x, p, w_in, diff_lq1, diff_lk1, diff_lq2, diff_lk2, diff_subln_g, nsa_pos_k, nsa_pos_v,
           nsa_phi_k1, nsa_phi_k2, nsa_phi_v1, nsa_phi_v2, w_br_diff, w_br_nsa, w_mgate, b_mgate, w_o,
           ln1_g, ln1_b, w_router, b_router, w_e1, b_e1, w_e2, b_e2, w_ple_gate, b_ple_gate,
           w_ple_proj, ln2_g, ln2_b):
    b, s, d = x.shape
    depth = w_in.shape[0]
    alpha = (2.0 * depth) ** 0.25
    x2 = x.reshape(b * s, d)
    for i in range(depth):
        lambda_init = 0.8 - 0.6 * math.exp(-0.3 * i)
        o_diff, o_nsa = _attention_block(
            x2, b, s, w_in[i],
            (diff_lq1[i], diff_lk1[i], diff_lq2[i], diff_lk2[i], diff_subln_g[i]),
            (nsa_pos_k[i], nsa_pos_v[i], nsa_phi_k1[i], nsa_phi_k2[i], nsa_phi_v1[i], nsa_phi_v2[i]),
            lambda_init)
        mixed = _mix(x2, o_diff, o_nsa, w_mgate[i], b_mgate[i], w_br_diff[i], w_br_nsa[i])
        h, top_e, gate, rank, counts = _post_attn(mixed, x2, w_o[i], ln1_g[i], ln1_b[i],
                                                  w_router[i], b_router[i], alpha=alpha)
        dest, y = _moe_block(h, top_e, gate, rank, counts, w_e1[i], b_e1[i], w_e2[i], b_e2[i])
        x2 = _final(h, gate, dest, y, p[i].reshape(b * s, -1), w_ple_gate[i], b_ple_gate[i],
                    w_ple_proj[i], ln2_g[i], ln2_b[i], alpha=alpha)
    return x2.reshape(b, s, d)
```

```python
import functools
import math

import numpy as np
import jax
import jax.numpy as jnp
from jax import lax
from jax.experimental import pallas as pl
from jax.experimental.pallas import tpu as pltpu

D_MODEL = 1024
DIFF_HEADS = 8
DIFF_HD = 64
NSA_HEADS = 16
NSA_KV = 4
NSA_HD = 64
HPG = NSA_HEADS // NSA_KV
CMP_LEN = 32
CMP_STRIDE = 16
SLC_LEN = 64
SLC_SHIFT = 6
SLC_TOPN = 16
WIN = 512
PHI_HID = 256
N_EXPERTS = 32
TOP_K = 4
D_FF = 1024
SWIGLU_LIMIT = 7.0
SWIGLU_ALPHA = 1.702
MOE_BLOCK = 512
PLE_DIM = 256
LN_EPS = 1e-5
NEG = -1e30
BIG = 1e30
LOWEST = -3e38
LOG2E = 1.4426950408889634

DIFF_QK = DIFF_HEADS * 2 * DIFF_HD
DIFF_V = DIFF_HEADS * 2 * DIFF_HD
NSA_Q = NSA_HEADS * NSA_HD
NSA_KVW = NSA_KV * NSA_HD
NSA_GATES = NSA_HEADS * 3
IN_SIZES = (DIFF_QK, DIFF_QK, DIFF_V, NSA_Q) + (NSA_KVW,) * 6 + (NSA_GATES,)
C_MAIN = sum(IN_SIZES[:-1])

VMEM_LIMIT = 48 * 1024 * 1024
EXPERT_VMEM_LIMIT = 56 * 1024 * 1024

BF16 = jnp.bfloat16
F32 = jnp.float32


def _alibi_slopes_log2(n):
    return (2.0 ** (-8.0 * np.arange(1, n + 1) / n) * LOG2E).astype(np.float32)


def _cparams(sem, vmem=VMEM_LIMIT):
    return pltpu.CompilerParams(dimension_semantics=sem, vmem_limit_bytes=vmem)


PROJ_CHUNK = 512
GATE_PAD = 128
C_QKV = 2 * DIFF_QK + DIFF_V + NSA_Q
N_KV = (C_MAIN - C_QKV) // NSA_HD


def _proj_kernel(x_ref, w_ref, s_ref, o_ref, kv_ref, g_ref):
    xb = x_ref[...].astype(BF16)
    for c in range(C_MAIN // PROJ_CHUNK):
        cs = slice(c * PROJ_CHUNK, (c + 1) * PROJ_CHUNK)
        acc = jnp.dot(xb, w_ref[:, cs], preferred_element_type=F32)
        if c < C_QKV // PROJ_CHUNK:
            o_ref[:, cs] = (acc * s_ref[:, cs]).astype(o_ref.dtype)
        else:
            per_chunk = PROJ_CHUNK // NSA_HD
            for j in range(per_chunk):
                stream = (c - C_QKV // PROJ_CHUNK) * per_chunk + j
                kv_ref[stream] = acc[:, j * NSA_HD:(j + 1) * NSA_HD].astype(kv_ref.dtype)
    gl = jnp.dot(xb, w_ref[:, C_MAIN:], preferred_element_type=F32)
    g_ref[...] = jax.nn.sigmoid(gl)


def _input_projection(x2, w_in, *, tm=512):
    m, k = x2.shape
    tm = min(tm, m)
    assert m % tm == 0 and C_MAIN % PROJ_CHUNK == 0 and C_QKV % PROJ_CHUNK == 0
    w = jnp.pad(w_in, ((0, 0), (0, GATE_PAD - NSA_GATES))).astype(BF16)
    scale = np.ones((1, C_QKV), np.float32)
    scale[0, :DIFF_QK] = DIFF_HD ** -0.5 * LOG2E
    scale[0, C_QKV - NSA_Q:] = NSA_HD ** -0.5 * LOG2E
    n_all = C_MAIN + GATE_PAD
    return pl.pallas_call(
        _proj_kernel,
        grid=(m // tm,),
        in_specs=[pl.BlockSpec((tm, k), lambda i: (i, 0)),
                  pl.BlockSpec((k, n_all), lambda i: (0, 0)),
                  pl.BlockSpec((1, C_QKV), lambda i: (0, 0))],
        out_specs=[pl.BlockSpec((tm, C_QKV), lambda i: (i, 0)),
                   pl.BlockSpec((N_KV, tm, NSA_HD), lambda i: (0, i, 0)),
                   pl.BlockSpec((tm, GATE_PAD), lambda i: (i, 0))],
        out_shape=[jax.ShapeDtypeStruct((m, C_QKV), BF16),
                   jax.ShapeDtypeStruct((N_KV, m, NSA_HD), BF16),
                   jax.ShapeDtypeStruct((m, GATE_PAD), F32)],
        compiler_params=_cparams(("parallel",)),
        name="input_projection",
    )(x2, w, jnp.asarray(scale))


LANES = 128


def _online_update(s, v, m_old, l_old, acc_old):
    tk = s.shape[1]
    hd = acc_old.shape[1]
    m_new = jnp.maximum(m_old, jnp.max(s, axis=-1, keepdims=True))
    alpha = jnp.exp2(m_old - m_new)
    p = jnp.exp2(s - jnp.concatenate([m_new] * (tk // LANES), axis=1))
    psum = p[:, :LANES]
    for c in range(1, tk // LANES):
        psum = psum + p[:, c * LANES:(c + 1) * LANES]
    l_new = alpha * l_old + psum
    acc_new = alpha[:, :hd] * acc_old + jnp.dot(p.astype(BF16), v, preferred_element_type=F32)
    return m_new, l_new, acc_new


def _online_step(s, v, m_ref, l_ref, acc_ref):
    m_new, l_new, acc_new = _online_update(s, v, m_ref[...], l_ref[...], acc_ref[...])
    m_ref[...] = m_new
    l_ref[...] = l_new
    acc_ref[...] = acc_new


def _row_total(l):
    return jnp.sum(l, axis=-1, keepdims=True)


def _tri_steps(n):
    qi, ki = [], []
    for q in range(n):
        for k in range(q + 1):
            qi.append(q)
            ki.append(k)
    return np.asarray(qi, np.int32), np.asarray(ki, np.int32)


def _diff_kernel(qi_ref, ki_ref, sl_ref, q_ref, k_ref, v_ref,
                 lq1_ref, lk1_ref, lq2_ref, lk2_ref, g_ref, o_ref,
                 qm0, qm1, m0, l0, a0, m1, l1, a1, *, t, lambda_init):
    h = pl.program_id(1)
    step = pl.program_id(2)
    qt = qi_ref[step]
    kt = ki_ref[step]
    slope = sl_ref[h]

    @pl.when(kt == 0)
    def _():
        q = q_ref[0]
        lane = lax.broadcasted_iota(jnp.int32, q.shape, 1)
        qm0[...] = jnp.where(lane < DIFF_HD, q, jnp.zeros_like(q))
        qm1[...] = jnp.where(lane >= DIFF_HD, q, jnp.zeros_like(q))
        for m, l, a in ((m0, l0, a0), (m1, l1, a1)):
            m[...] = jnp.full(m.shape, NEG, F32)
            l[...] = jnp.zeros(l.shape, F32)
            a[...] = jnp.zeros(a.shape, F32)

    col = lax.broadcasted_iota(jnp.int32, (1, t), 1)
    bias = slope * ((kt - qt) * t + col).astype(F32)
    k = k_ref[0]
    v = v_ref[0]

    def scores(qm):
        return lax.dot_general(qm[...], k, (((1,), (1,)), ((), ())), preferred_element_type=F32)

    @pl.when(kt < qt)
    def _():
        _online_step(scores(qm0) + bias, v, m0, l0, a0)
        _online_step(scores(qm1) + bias, v, m1, l1, a1)

    @pl.when(kt == qt)
    def _():
        row = lax.broadcasted_iota(jnp.int32, (t, t), 0)
        colf = lax.broadcasted_iota(jnp.int32, (t, t), 1)
        bias_c = jnp.where(colf <= row, bias, NEG)
        _online_step(scores(qm0) + bias_c, v, m0, l0, a0)
        _online_step(scores(qm1) + bias_c, v, m1, l1, a1)
        lam = (jnp.exp(jnp.sum(lq1_ref[...] * lk1_ref[...], axis=-1, keepdims=True))
               - jnp.exp(jnp.sum(lq2_ref[...] * lk2_ref[...], axis=-1, keepdims=True))
               + lambda_init)
        o = a0[...] / _row_total(l0[...]) - lam * (a1[...] / _row_total(l1[...]))
        ms = jnp.mean(o * o, axis=-1, keepdims=True)
        o = o * lax.rsqrt(ms + LN_EPS) * g_ref[...] * (1.0 - lambda_init)
        o_ref[0] = o.astype(o_ref.dtype)


def _diff_attention(proj3, lq1, lk1, lq2, lk2, g, lambda_init, *, t=1024):
    b, s, _ = proj3.shape
    t = min(t, s)
    qi, ki = _tri_steps(s // t)
    w = 2 * DIFF_HD
    slopes = jnp.asarray(_alibi_slopes_log2(DIFF_HEADS))
    vec = lambda a: a.reshape(1, -1).astype(F32)
    small = lambda c: pl.BlockSpec((1, c), lambda b_, h, st, qi, ki, sl: (0, 0))
    grid_spec = pltpu.PrefetchScalarGridSpec(
        num_scalar_prefetch=3,
        grid=(b, DIFF_HEADS, len(qi)),
        in_specs=[pl.BlockSpec((1, t, w), lambda b_, h, st, qi, ki, sl: (b_, qi[st], h)),
                  pl.BlockSpec((1, t, w), lambda b_, h, st, qi, ki, sl: (b_, ki[st], DIFF_HEADS + h)),
                  pl.BlockSpec((1, t, w), lambda b_, h, st, qi, ki, sl: (b_, ki[st], 2 * DIFF_HEADS + h)),
                  small(DIFF_HD), small(DIFF_HD), small(DIFF_HD), small(DIFF_HD), small(w)],
        out_specs=pl.BlockSpec((1, t, w), lambda b_, h, st, qi, ki, sl: (b_, qi[st], h)),
        scratch_shapes=[pltpu.VMEM((t, w), BF16), pltpu.VMEM((t, w), BF16),
                        pltpu.VMEM((t, LANES), F32), pltpu.VMEM((t, LANES), F32), pltpu.VMEM((t, w), F32),
                        pltpu.VMEM((t, LANES), F32), pltpu.VMEM((t, LANES), F32), pltpu.VMEM((t, w), F32)],
    )
    return pl.pallas_call(
        functools.partial(_diff_kernel, t=t, lambda_init=lambda_init),
        grid_spec=grid_spec,
        out_shape=jax.ShapeDtypeStruct((b, s, DIFF_V), BF16),
        compiler_params=_cparams(("parallel", "parallel", "arbitrary")),
        name="diff_attention",
    )(jnp.asarray(qi), jnp.asarray(ki), slopes, proj3, proj3, proj3,
      vec(lq1), vec(lk1), vec(lq2), vec(lk2), vec(g))


def _compress_kernel(a_ref, w1_ref, w2_ref, pos_ref, o_ref):
    a = a_ref[0, 0]
    half = CMP_STRIDE * NSA_HD
    top = jnp.dot(a, w1_ref[:half, :], preferred_element_type=F32)
    bot = jnp.dot(a, w1_ref[half:, :], preferred_element_type=F32)
    pos = pos_ref[...]
    pos_hi = pos.astype(BF16)
    pos_lo = (pos - pos_hi.astype(F32)).astype(BF16)
    c = (jnp.dot(pos_hi, w1_ref[...], preferred_element_type=F32)
         + jnp.dot(pos_lo, w1_ref[...], preferred_element_type=F32))[0:1]
    n = a.shape[0]
    hid = top + pltpu.roll(bot, n - 1, axis=0) + c
    hid = hid * jax.nn.sigmoid(hid)
    o_ref[0, 0] = jnp.dot(hid.astype(BF16), w2_ref[...], preferred_element_type=F32).astype(o_ref.dtype)


def _compress(t_halves, pos, w1, w2):
    b, g, nh, _ = t_halves.shape
    pos8 = jnp.broadcast_to(pos.reshape(1, -1).astype(F32), (8, CMP_LEN * NSA_HD))
    return pl.pallas_call(
        _compress_kernel,
        grid=(b, g),
        in_specs=[pl.BlockSpec((1, 1, nh, CMP_STRIDE * NSA_HD), lambda i, j: (i, j, 0, 0)),
                  pl.BlockSpec((CMP_LEN * NSA_HD, PHI_HID), lambda i, j: (0, 0)),
                  pl.BlockSpec((PHI_HID, NSA_HD), lambda i, j: (0, 0)),
                  pl.BlockSpec((8, CMP_LEN * NSA_HD), lambda i, j: (0, 0))],
        out_specs=pl.BlockSpec((1, 1, nh, NSA_HD), lambda i, j: (i, j, 0, 0)),
        out_shape=jax.ShapeDtypeStruct((b, g, nh, NSA_HD), BF16),
        compiler_params=_cparams(("parallel", "parallel")),
        name="nsa_compress",
    )(t_halves, w1.astype(BF16), w2.astype(BF16), pos8)


SEL_LANES = 64


def _cmp_select_kernel(sl_ref, q_ref, kc_ref, vc_ref, o_ref, qs_ref, *, tq, n_cmp_pad, n_slc, n_top):
    g = pl.program_id(1)
    qt = pl.program_id(2)
    t_col = qt * tq + lax.broadcasted_iota(jnp.int32, (tq, 1), 0)
    cmp_pos = lax.broadcasted_iota(jnp.int32, (1, n_cmp_pad), 1) * CMP_STRIDE + (CMP_LEN - 1)
    mask_add = jnp.where(t_col >= cmp_pos, 0.0, NEG)
    row_ok = jnp.where(t_col >= CMP_LEN - 1, 1.0, 0.0)
    cmp_posf = (cmp_pos - qt * tq).astype(F32)
    kc = kc_ref[0, 0]
    vc = vc_ref[0, 0]
    psum = jnp.zeros((tq, n_cmp_pad), F32)
    for hh in range(HPG):
        slope = sl_ref[g * HPG + hh]
        q_h = q_ref[0][:, hh * NSA_HD:(hh + 1) * NSA_HD]
        s = lax.dot_general(q_h, kc, (((1,), (1,)), ((), ())), preferred_element_type=F32)
        s = s + (mask_add + slope * cmp_posf)
        m = jnp.max(s, axis=-1, keepdims=True)
        p = jnp.exp2(s - m)
        l = jnp.sum(p, axis=-1, keepdims=True)
        p = p * (row_ok / l)
        o_ref[0, hh] = jnp.dot(p.astype(BF16), vc, preferred_element_type=F32)
        psum = psum + p
    c_lo = lax.broadcasted_iota(jnp.int32, (n_cmp_pad, 128), 0) * CMP_STRIDE
    s_lo = lax.broadcasted_iota(jnp.int32, (n_cmp_pad, 128), 1) * SLC_LEN
    ov = jnp.minimum(c_lo + CMP_LEN, s_lo + SLC_LEN) - jnp.maximum(c_lo, s_lo)
    cmap = (jnp.maximum(ov, 0).astype(F32) * (1.0 / CMP_LEN)).astype(BF16)
    ps_hi = psum.astype(BF16)
    ps_lo = (psum - ps_hi.astype(F32)).astype(BF16)
    imp = (jnp.dot(ps_hi, cmap, preferred_element_type=F32)
           + jnp.dot(ps_lo, cmap, preferred_element_type=F32))
    imp_t = jnp.transpose(imp)[:n_slc]
    blk = lax.broadcasted_iota(jnp.int32, (n_slc, tq), 0)
    t_row = qt * tq + lax.broadcasted_iota(jnp.int32, (n_slc, tq), 1)
    cur = lax.shift_right_logical(t_row, SLC_SHIFT)
    forced = (blk == 0) | (blk == cur) | (blk == cur - 1)
    x = jnp.where(forced, BIG, jnp.where(blk <= cur, imp_t, NEG))
    sel = jnp.zeros((n_slc, tq), F32)
    for _ in range(n_top):
        mx = jnp.max(x, axis=0, keepdims=True)
        idx = jnp.min(jnp.where(x == mx, blk, n_slc), axis=0, keepdims=True)
        hit = blk == idx
        sel = jnp.where(hit, 1.0, sel)
        x = jnp.where(hit, LOWEST, x)
    if n_slc < 128:
        sel = jnp.concatenate([sel, jnp.zeros((128 - n_slc, tq), F32)], axis=0)
    sel_q = jnp.transpose(sel)[:, :SEL_LANES]
    sel_bias = jnp.where(sel_q > 0.5, 0.0, NEG)
    for hh in range(HPG):
        q_h = q_ref[0][:, hh * NSA_HD:(hh + 1) * NSA_HD]
        qs_ref[0, hh] = jnp.concatenate([q_h.astype(F32), sel_bias], axis=1).astype(qs_ref.dtype)


def _cmp_select(qkv3, k_cmp, v_cmp, *, tq=512):
    b, s, _ = qkv3.shape
    tq = min(tq, s)
    n_cmp_pad = k_cmp.shape[2]
    gw = HPG * NSA_HD
    q_blk0 = (C_QKV - NSA_Q) // gw
    n_slc = s // SLC_LEN
    n_top = min(SLC_TOPN, n_slc)
    assert n_slc <= SEL_LANES
    slopes = jnp.asarray(_alibi_slopes_log2(NSA_HEADS))
    grid_spec = pltpu.PrefetchScalarGridSpec(
        num_scalar_prefetch=1,
        grid=(b, NSA_KV, s // tq),
        in_specs=[pl.BlockSpec((1, tq, gw), lambda i, j, k, sl: (i, k, q_blk0 + j)),
                  pl.BlockSpec((1, 1, n_cmp_pad, NSA_HD), lambda i, j, k, sl: (j, i, 0, 0)),
                  pl.BlockSpec((1, 1, n_cmp_pad, NSA_HD), lambda i, j, k, sl: (j, i, 0, 0))],
        out_specs=[pl.BlockSpec((1, HPG, tq, NSA_HD), lambda i, j, k, sl: (i, j, k, 0)),
                   pl.BlockSpec((1, HPG, tq, NSA_HD + SEL_LANES), lambda i, j, k, sl: (i, j, k, 0))],
    )
    return pl.pallas_call(
        functools.partial(_cmp_select_kernel, tq=tq, n_cmp_pad=n_cmp_pad, n_slc=n_slc, n_top=n_top),
        grid_spec=grid_spec,
        out_shape=[jax.ShapeDtypeStruct((b, NSA_HEADS, s, NSA_HD), F32),
                   jax.ShapeDtypeStruct((b, NSA_HEADS, s, NSA_HD + SEL_LANES), BF16)],
        compiler_params=_cparams(("parallel", "parallel", "parallel")),
        name="nsa_cmp_select",
    )(slopes, qkv3, k_cmp, v_cmp)


def _slc_kernel(qi_ref, ki_ref, sl_ref, q_ref, k_ref, v_ref, o_ref, m_sc, l_sc, a_sc, *, t):
    g = pl.program_id(1)
    step = pl.program_id(2)
    qt = qi_ref[step]
    kt = ki_ref[step]

    @pl.when(kt == 0)
    def _():
        m_sc[...] = jnp.full(m_sc.shape, NEG, F32)
        l_sc[...] = jnp.zeros(l_sc.shape, F32)
        a_sc[...] = jnp.zeros(a_sc.shape, F32)

    col = lax.broadcasted_iota(jnp.int32, (1, t), 1)
    rel = ((kt - qt) * t + col).astype(F32)
    blk_of_key = lax.shift_right_logical(kt * t + lax.broadcasted_iota(jnp.int32, (t, SEL_LANES), 0), SLC_SHIFT)
    onehot = jnp.where(blk_of_key == lax.broadcasted_iota(jnp.int32, (t, SEL_LANES), 1), 1.0, 0.0)
    k = jnp.concatenate([k_ref[0, 0].astype(F32), onehot], axis=1).astype(BF16)
    v = v_ref[0, 0]

    def sweep(mask_add):
        for hh in range(HPG):
            s = lax.dot_general(q_ref[0, hh], k, (((1,), (1,)), ((), ())), preferred_element_type=F32)
            bias = sl_ref[g * HPG + hh] * rel
            s = s + (bias if mask_add is None else mask_add + bias)
            m_new, l_new, a_new = _online_update(s, v, m_sc[hh], l_sc[hh], a_sc[hh])
            m_sc[hh] = m_new
            l_sc[hh] = l_new
            a_sc[hh] = a_new

    @pl.when(kt < qt)
    def _():
        sweep(None)

    @pl.when(kt == qt)
    def _():
        row = lax.broadcasted_iota(jnp.int32, (t, t), 0)
        colf = lax.broadcasted_iota(jnp.int32, (t, t), 1)
        sweep(jnp.where(colf <= row, 0.0, NEG))
        for hh in range(HPG):
            o_ref[0, hh] = a_sc[hh] / _row_total(l_sc[hh])


def _slc_attention(qs, kv4, *, t=1024):
    b, _, s, w = qs.shape
    t = min(t, s)
    qi, ki = _tri_steps(s // t)
    slopes = jnp.asarray(_alibi_slopes_log2(NSA_HEADS))
    grid_spec = pltpu.PrefetchScalarGridSpec(
        num_scalar_prefetch=3,
        grid=(b, NSA_KV, len(qi)),
        in_specs=[pl.BlockSpec((1, HPG, t, w), lambda i, g, st, qi, ki, sl: (i, g, qi[st], 0)),
                  pl.BlockSpec((1, 1, t, NSA_HD), lambda i, g, st, qi, ki, sl: (2 * NSA_KV + g, i, ki[st], 0)),
                  pl.BlockSpec((1, 1, t, NSA_HD), lambda i, g, st, qi, ki, sl: (3 * NSA_KV + g, i, ki[st], 0))],
        out_specs=pl.BlockSpec((1, HPG, t, NSA_HD), lambda i, g, st, qi, ki, sl: (i, g, qi[st], 0)),
        scratch_shapes=[pltpu.VMEM((HPG, t, LANES), F32), pltpu.VMEM((HPG, t, LANES), F32),
                        pltpu.VMEM((HPG, t, NSA_HD), F32)],
    )
    return pl.pallas_call(
        functools.partial(_slc_kernel, t=t),
        grid_spec=grid_spec,
        out_shape=jax.ShapeDtypeStruct((b, NSA_HEADS, s, NSA_HD), F32),
        compiler_params=_cparams(("parallel", "parallel", "arbitrary")),
        name="nsa_slc_attention",
    )(jnp.asarray(qi), jnp.asarray(ki), slopes, qs, kv4, kv4)


def _win_kernel(sl_ref, q_ref, *rest, t):
    nblk = WIN // t
    k_refs = rest[:nblk + 1]
    v_refs = rest[nblk + 1:2 * nblk + 2]
    oc_ref, os_ref, g_ref, o_ref = rest[2 * nblk + 2:]
    g = pl.program_id(1)
    qt = pl.program_id(2)
    row = lax.broadcasted_iota(jnp.int32, (t, t), 0)
    col = lax.broadcasted_iota(jnp.int32, (t, t), 1)
    colr = lax.broadcasted_iota(jnp.int32, (1, t), 1)
    refs = tuple(zip(k_refs, v_refs))
    masks = []
    for j in range(len(refs)):
        off = j - nblk
        if off == -nblk:
            masks.append(jnp.where((col > row) & (qt + off >= 0), 0.0, NEG))
        elif off < 0:
            masks.append(jnp.where(qt + off >= 0, 0.0, NEG))
        else:
            masks.append(jnp.where(col <= row, 0.0, NEG))
    outs = []
    for hh in range(HPG):
        slope = sl_ref[g * HPG + hh]
        q = q_ref[0, hh][:, :NSA_HD]
        m = jnp.full((t, LANES), NEG, F32)
        l = jnp.zeros((t, LANES), F32)
        acc = jnp.zeros((t, NSA_HD), F32)
        for j, (k_ref, v_ref) in enumerate(refs):
            off = j - nblk
            s = lax.dot_general(q, k_ref[0, 0], (((1,), (1,)), ((), ())), preferred_element_type=F32)
            s = s + (masks[j] + slope * (off * t + colr).astype(F32))
            m, l, acc = _online_update(s, v_ref[0, 0], m, l, acc)
        o_win = acc / _row_total(l)
        gt = g_ref[0, hh]
        outs.append(gt[:, 0:1] * oc_ref[0, hh] + gt[:, 1:2] * os_ref[0, hh] + gt[:, 2:3] * o_win)
    o_ref[0] = jnp.concatenate(outs, axis=1).astype(o_ref.dtype)


def _win_combine(qs, kv4, o_cmp, o_slc, gates, *, t=512):
    b, _, s, w = qs.shape
    t = min(t, s)
    assert WIN % t == 0
    nblk = WIN // t
    slopes = jnp.asarray(_alibi_slopes_log2(NSA_HEADS))
    kv = lambda kind, off: pl.BlockSpec(
        (1, 1, t, NSA_HD), lambda i, g, qt, sl: (kind * NSA_KV + g, i, jnp.maximum(qt + off, 0), 0))
    per_head = lambda c: pl.BlockSpec((1, HPG, t, c), lambda i, g, qt, sl: (i, g, qt, 0))
    grid_spec = pltpu.PrefetchScalarGridSpec(
        num_scalar_prefetch=1,
        grid=(b, NSA_KV, s // t),
        in_specs=([per_head(w)] + [kv(4, off) for off in range(-nblk, 1)]
                  + [kv(5, off) for off in range(-nblk, 1)]
                  + [per_head(NSA_HD), per_head(NSA_HD), per_head(3)]),
        out_specs=pl.BlockSpec((1, t, HPG * NSA_HD), lambda i, g, qt, sl: (i, qt, g)),
    )
    return pl.pallas_call(
        functools.partial(_win_kernel, t=t),
        grid_spec=grid_spec,
        out_shape=jax.ShapeDtypeStruct((b, s, NSA_Q), BF16),
        compiler_params=_cparams(("parallel", "parallel", "parallel")),
        name="nsa_win_combine",
    )(slopes, qs, *([kv4] * (2 * nblk + 2)), o_cmp, o_slc, gates)


def _mix_kernel(x_ref, od_ref, on_ref, wg1_ref, wg2_ref, bg1_ref, bg2_ref, wd_ref, wn_ref, o_ref, xb_ref):
    @pl.when(pl.program_id(1) == 0)
    def _():
        xb_ref[...] = x_ref[...].astype(BF16)

    xb = xb_ref[...]
    g1 = jax.nn.sigmoid(jnp.dot(xb, wg1_ref[...], preferred_element_type=F32) + bg1_ref[...])
    g2 = jax.nn.sigmoid(jnp.dot(xb, wg2_ref[...], preferred_element_type=F32) + bg2_ref[...])
    bd = jnp.dot(od_ref[...], wd_ref[...], preferred_element_type=F32)
    bn = jnp.dot(on_ref[...], wn_ref[...], preferred_element_type=F32)
    o_ref[...] = (g1 * bd + g2 * bn).astype(o_ref.dtype)


def _mix(x2, o_diff, o_nsa, w_mgate, b_mgate, w_br_diff, w_br_nsa, *, tm=512, tn=512):
    m, d = x2.shape
    tm = min(tm, m)
    nj = d // tn
    wg = w_mgate.astype(BF16)
    bg = b_mgate.reshape(1, -1).astype(F32)
    row = pl.BlockSpec((tm, d), lambda i, j: (i, 0))
    return pl.pallas_call(
        _mix_kernel,
        grid=(m // tm, nj),
        in_specs=[row, row, row,
                  pl.BlockSpec((d, tn), lambda i, j: (0, j)),
                  pl.BlockSpec((d, tn), lambda i, j: (0, j + nj)),
                  pl.BlockSpec((1, tn), lambda i, j: (0, j)),
                  pl.BlockSpec((1, tn), lambda i, j: (0, j + nj)),
                  pl.BlockSpec((d, tn), lambda i, j: (0, j)),
                  pl.BlockSpec((d, tn), lambda i, j: (0, j))],
        out_specs=pl.BlockSpec((tm, tn), lambda i, j: (i, j)),
        out_shape=jax.ShapeDtypeStruct((m, d), BF16),
        scratch_shapes=[pltpu.VMEM((tm, d), BF16)],
        compiler_params=_cparams(("parallel", "arbitrary")),
        name="branch_mix",
    )(x2, o_diff, o_nsa, wg, wg, bg, bg, w_br_diff.astype(BF16), w_br_nsa.astype(BF16))


def _layer_norm_rows(y, g, b):
    mu = jnp.mean(y, axis=-1, keepdims=True)
    yc = y - mu
    var = jnp.mean(yc * yc, axis=-1, keepdims=True)
    return yc * lax.rsqrt(var + LN_EPS) * g + b


def _split_bf16(a):
    hi = a.astype(BF16)
    lo = (a - hi.astype(F32)).astype(BF16)
    return hi, lo


def _post_attn_kernel(mx_ref, x_ref, wo_ref, g_ref, b_ref, wr_hi_ref, wr_lo_ref, br_ref,
                      h_ref, e_ref, gate_ref, rank_ref, cnt_ref, run_ref, *, tm, alpha):
    i = pl.program_id(0)

    @pl.when(i == 0)
    def _():
        run_ref[...] = jnp.zeros(run_ref.shape, F32)

    y = alpha * x_ref[...] + jnp.dot(mx_ref[...], wo_ref[...], preferred_element_type=F32)
    h = _layer_norm_rows(y, g_ref[...], b_ref[...])
    h_ref[...] = h
    h_hi, h_lo = _split_bf16(h)
    logits = (jnp.dot(h_hi, wr_hi_ref[...], preferred_element_type=F32)
              + jnp.dot(h_hi, wr_lo_ref[...], preferred_element_type=F32)
              + jnp.dot(h_lo, wr_hi_ref[...], preferred_element_type=F32)) + br_ref[...]
    lane = lax.broadcasted_iota(jnp.int32, (tm, N_EXPERTS), 1)
    lane4 = lax.broadcasted_iota(jnp.int32, (tm, TOP_K), 1)
    xl = logits
    vals, idxs, hits = [], [], []
    for _ in range(TOP_K):
        mx = jnp.max(xl, axis=-1, keepdims=True)
        idx = jnp.min(jnp.where(xl == mx, lane, N_EXPERTS), axis=-1, keepdims=True)
        hit = lane == idx
        vals.append(mx)
        idxs.append(idx)
        hits.append(hit)
        xl = jnp.where(hit, LOWEST, xl)
    ex = [jnp.exp(v - vals[0]) for v in vals]
    den = ex[0] + ex[1] + ex[2] + ex[3]
    member = jnp.zeros((tm, N_EXPERTS), F32)
    for hit in hits:
        member = jnp.where(hit, 1.0, member)
    r_i = lax.broadcasted_iota(jnp.int32, (tm, tm), 0)
    c_i = lax.broadcasted_iota(jnp.int32, (tm, tm), 1)
    tri = jnp.where(c_i < r_i, 1.0, 0.0).astype(BF16)
    before = jnp.dot(tri, member.astype(BF16), preferred_element_type=F32) + run_ref[...]
    e_out = jnp.zeros((tm, TOP_K), jnp.int32)
    g_out = jnp.zeros((tm, TOP_K), F32)
    r_out = jnp.zeros((tm, TOP_K), jnp.int32)
    for k in range(TOP_K):
        rk = jnp.sum(jnp.where(hits[k], before, 0.0), axis=-1, keepdims=True)
        e_out = jnp.where(lane4 == k, idxs[k], e_out)
        g_out = jnp.where(lane4 == k, ex[k] / den, g_out)
        r_out = jnp.where(lane4 == k, rk.astype(jnp.int32), r_out)
    e_ref[...] = e_out
    gate_ref[...] = g_out
    rank_ref[...] = r_out
    run_ref[...] = run_ref[...] + jnp.sum(member, axis=0, keepdims=True)
    cnt_ref[...] = run_ref[...].astype(jnp.int32)


def _post_attn(mixed, x2, w_o, ln_g, ln_b, w_router, b_router, *, alpha, tm=256):
    m, d = x2.shape
    tm = min(tm, m)
    wr_hi, wr_lo = _split_bf16(w_router.astype(F32))
    vec = lambda a: a.reshape(1, -1).astype(F32)
    full = lambda r, c: pl.BlockSpec((r, c), lambda i: (0, 0))
    rows = lambda c: pl.BlockSpec((tm, c), lambda i: (i, 0))
    return pl.pallas_call(
        functools.partial(_post_attn_kernel, tm=tm, alpha=alpha),
        grid=(m // tm,),
        in_specs=[rows(d), rows(d), full(d, d), full(1, d), full(1, d),
                  full(d, N_EXPERTS), full(d, N_EXPERTS), full(1, N_EXPERTS)],
        out_specs=[rows(d), rows(TOP_K), rows(TOP_K), rows(TOP_K), full(1, N_EXPERTS)],
        out_shape=[jax.ShapeDtypeStruct((m, d), F32),
                   jax.ShapeDtypeStruct((m, TOP_K), jnp.int32),
                   jax.ShapeDtypeStruct((m, TOP_K), F32),
                   jax.ShapeDtypeStruct((m, TOP_K), jnp.int32),
                   jax.ShapeDtypeStruct((1, N_EXPERTS), jnp.int32)],
        scratch_shapes=[pltpu.VMEM((1, N_EXPERTS), F32)],
        compiler_params=_cparams(("arbitrary",)),
        name="post_attn_router",
    )(mixed, x2, w_o.astype(BF16), vec(ln_g), vec(ln_b), wr_hi, wr_lo, vec(b_router))


def _dest_kernel(ps_ref, e_ref, r_ref, o_ref):
    e = e_ref[...]
    base = jnp.zeros(e.shape, jnp.int32)
    for j in range(N_EXPERTS):
        base = jnp.where(e == j, ps_ref[j], base)
    o_ref[...] = base + r_ref[...]


def _dest(top_e, rank, pstart, *, tm=2048):
    m = top_e.shape[0]
    tm = min(tm, m)
    grid_spec = pltpu.PrefetchScalarGridSpec(
        num_scalar_prefetch=1,
        grid=(m // tm,),
        in_specs=[pl.BlockSpec((tm, TOP_K), lambda i, ps: (i, 0)),
                  pl.BlockSpec((tm, TOP_K), lambda i, ps: (i, 0))],
        out_specs=pl.BlockSpec((tm, TOP_K), lambda i, ps: (i, 0)),
    )
    return pl.pallas_call(
        _dest_kernel, grid_spec=grid_spec,
        out_shape=jax.ShapeDtypeStruct((m, TOP_K), jnp.int32),
        compiler_params=_cparams(("parallel",)),
        name="moe_dest",
    )(pstart, top_e, rank)


DMA_UNROLL = 8


def _dispatch_kernel(zf_ref, dest_ref, h_ref, xs_ref, zeros, sem, zsem, *, tm, n_blocks):
    @pl.when(pl.program_id(0) == 0)
    def _():
        zeros[...] = jnp.zeros(zeros.shape, zeros.dtype)

        def zero_copy(blk):
            start = pl.multiple_of(blk * MOE_BLOCK, MOE_BLOCK)
            return pltpu.make_async_copy(zeros, xs_ref.at[pl.ds(start, MOE_BLOCK), :], zsem)

        def zstart(blk, carry):
            @pl.when(zf_ref[blk] > 0)
            def _():
                zero_copy(blk).start()
            return carry

        def zwait(blk, carry):
            @pl.when(zf_ref[blk] > 0)
            def _():
                zero_copy(blk).wait()
            return carry

        lax.fori_loop(0, n_blocks, zstart, 0)
        lax.fori_loop(0, n_blocks, zwait, 0)

    def row_copy(r, k):
        return pltpu.make_async_copy(h_ref.at[pl.ds(r, 1), :],
                                     xs_ref.at[pl.ds(dest_ref[r * TOP_K + k], 1), :], sem)

    def start(i, carry):
        for u in range(DMA_UNROLL // TOP_K):
            for k in range(TOP_K):
                row_copy(i * (DMA_UNROLL // TOP_K) + u, k).start()
        return carry

    def wait(i, carry):
        for u in range(DMA_UNROLL // TOP_K):
            for k in range(TOP_K):
                row_copy(i * (DMA_UNROLL // TOP_K) + u, k).wait()
        return carry

    lax.fori_loop(0, tm * TOP_K // DMA_UNROLL, start, 0)
    lax.fori_loop(0, tm * TOP_K // DMA_UNROLL, wait, 0)


def _dispatch(h, dest_flat, zero_flag, n_pad, *, tm=256):
    m, d = h.shape
    tm = min(tm, m)
    assert m % tm == 0 and (tm * TOP_K) % DMA_UNROLL == 0 and DMA_UNROLL % TOP_K == 0
    n_blocks = n_pad // MOE_BLOCK
    grid_spec = pltpu.PrefetchScalarGridSpec(
        num_scalar_prefetch=1,
        grid=(m // tm,),
        in_specs=[pl.BlockSpec((tm * TOP_K,), lambda i, zf: (i,), memory_space=pltpu.SMEM),
                  pl.BlockSpec((tm, d), lambda i, zf: (i, 0))],
        out_specs=pl.BlockSpec(memory_space=pl.ANY),
        scratch_shapes=[pltpu.VMEM((MOE_BLOCK, d), h.dtype), pltpu.SemaphoreType.DMA(()),
                        pltpu.SemaphoreType.DMA(())],
    )
    return pl.pallas_call(
        functools.partial(_dispatch_kernel, tm=tm, n_blocks=n_blocks),
        grid_spec=grid_spec,
        out_shape=jax.ShapeDtypeStruct((n_pad, d), h.dtype),
        compiler_params=_cparams(("arbitrary",)),
        name="moe_dispatch",
    )(zero_flag, dest_flat, h)


def _expert_kernel(be_ref, nu_ref, xs_ref, w1_ref, b1_ref, w2_ref, b2_ref, y_ref, w1b, w2b):
    i = pl.program_id(0)
    new_expert = (i == 0) | (be_ref[i] != be_ref[jnp.maximum(i - 1, 0)])

    @pl.when((i < nu_ref[0]) & new_expert)
    def _():
        w1b[...] = w1_ref[0].astype(BF16)
        w2b[...] = w2_ref[0].astype(BF16)

    @pl.when(i < nu_ref[0])
    def _():
        xb = xs_ref[...].astype(BF16)
        hb = jnp.dot(xb, w1b[...], preferred_element_type=F32) + b1_ref[0]
        glu = jnp.minimum(hb[:, :D_FF], SWIGLU_LIMIT)
        lin = jnp.clip(hb[:, D_FF:], -SWIGLU_LIMIT, SWIGLU_LIMIT)
        act = glu * jax.nn.sigmoid(SWIGLU_ALPHA * glu) * (lin + 1.0)
        y_ref[...] = jnp.dot(act.astype(BF16), w2b[...], preferred_element_type=F32) + b2_ref[0]

    @pl.when(i >= nu_ref[0])
    def _():
        y_ref[...] = jnp.zeros(y_ref.shape, y_ref.dtype)


def _experts(xs, blk_expert, n_used, w1, b1, w2, b2):
    n_pad, d = xs.shape
    n_blocks = n_pad // MOE_BLOCK
    last = lambda i, nu: jnp.minimum(i, nu[0] - 1)
    grid_spec = pltpu.PrefetchScalarGridSpec(
        num_scalar_prefetch=2,
        grid=(n_blocks,),
        in_specs=[pl.BlockSpec((MOE_BLOCK, d), lambda i, be, nu: (last(i, nu), 0)),
                  pl.BlockSpec((1, d, 2 * D_FF), lambda i, be, nu: (be[last(i, nu)], 0, 0)),
                  pl.BlockSpec((1, 1, 2 * D_FF), lambda i, be, nu: (be[last(i, nu)], 0, 0)),
                  pl.BlockSpec((1, D_FF, d), lambda i, be, nu: (be[last(i, nu)], 0, 0)),
                  pl.BlockSpec((1, 1, d), lambda i, be, nu: (be[last(i, nu)], 0, 0))],
        out_specs=pl.BlockSpec((MOE_BLOCK, d), lambda i, be, nu: (i, 0)),
        scratch_shapes=[pltpu.VMEM((d, 2 * D_FF), BF16), pltpu.VMEM((D_FF, d), BF16)],
    )
    return pl.pallas_call(
        _expert_kernel, grid_spec=grid_spec,
        out_shape=jax.ShapeDtypeStruct((n_pad, d), F32),
        compiler_params=_cparams(("arbitrary",), vmem=EXPERT_VMEM_LIMIT),
        name="moe_experts",
    )(blk_expert, n_used, xs, w1.astype(F32), b1.reshape(N_EXPERTS, 1, -1).astype(F32),
      w2.astype(F32), b2.reshape(N_EXPERTS, 1, -1).astype(F32))


def _final_kernel(dcur_ref, dnxt_ref, h_ref, gate_ref, p_ref, wpg_ref, bpg_ref, wpp_ref, g_ref, b_ref, y_ref,
                  o_ref, buf, sem, *, tm, alpha):
    i = pl.program_id(0)
    n = pl.num_programs(0)
    slot = lax.rem(i, 2)
    rows_per_iter = DMA_UNROLL // TOP_K

    def row_copy(dref, sl, r, k):
        return pltpu.make_async_copy(y_ref.at[pl.ds(dref[r * TOP_K + k], 1), :],
                                     buf.at[sl, k, pl.ds(r, 1), :], sem.at[sl])

    def issue(dref, sl):
        def body(it, carry):
            for u in range(rows_per_iter):
                for k in range(TOP_K):
                    row_copy(dref, sl, it * rows_per_iter + u, k).start()
            return carry
        lax.fori_loop(0, tm // rows_per_iter, body, 0)

    def drain(dref, sl):
        def body(it, carry):
            for u in range(rows_per_iter):
                for k in range(TOP_K):
                    row_copy(dref, sl, it * rows_per_iter + u, k).wait()
            return carry
        lax.fori_loop(0, tm // rows_per_iter, body, 0)

    @pl.when(i == 0)
    def _():
        issue(dcur_ref, slot)

    @pl.when(i + 1 < n)
    def _():
        issue(dnxt_ref, 1 - slot)

    h = h_ref[...]
    ple = (jax.nn.sigmoid(jnp.dot(h.astype(BF16), wpg_ref[...], preferred_element_type=F32) + bpg_ref[...])
           * jnp.dot(p_ref[...].astype(BF16), wpp_ref[...], preferred_element_type=F32))
    drain(dcur_ref, slot)
    gate = gate_ref[...]
    ffn = gate[:, 0:1] * buf[slot, 0]
    for k in range(1, TOP_K):
        ffn = ffn + gate[:, k:k + 1] * buf[slot, k]
    o_ref[...] = _layer_norm_rows(alpha * h + ffn + ple, g_ref[...], b_ref[...])


def _final(h, gate, dest_flat, y, p2, w_ple_gate, b_ple_gate, w_ple_proj, ln_g, ln_b, *, alpha, tm=256):
    m, d = h.shape
    tm = min(tm, m)
    n = m // tm
    vec = lambda a: a.reshape(1, -1).astype(F32)
    full = lambda r, c: pl.BlockSpec((r, c), lambda i: (0, 0))
    rows = lambda c: pl.BlockSpec((tm, c), lambda i: (i, 0))
    return pl.pallas_call(
        functools.partial(_final_kernel, tm=tm, alpha=alpha),
        grid=(n,),
        in_specs=[pl.BlockSpec((tm * TOP_K,), lambda i: (i,), memory_space=pltpu.SMEM),
                  pl.BlockSpec((tm * TOP_K,), lambda i: (jnp.minimum(i + 1, n - 1),), memory_space=pltpu.SMEM),
                  rows(d), rows(TOP_K), rows(PLE_DIM), full(d, d), full(1, d), full(PLE_DIM, d),
                  full(1, d), full(1, d), pl.BlockSpec(memory_space=pl.ANY)],
        out_specs=rows(d),
        out_shape=jax.ShapeDtypeStruct((m, d), F32),
        scratch_shapes=[pltpu.VMEM((2, TOP_K, tm, d), F32), pltpu.SemaphoreType.DMA((2,))],
        compiler_params=_cparams(("arbitrary",)),
        name="moe_combine_final",
    )(dest_flat, dest_flat, h, gate, p2, w_ple_gate.astype(BF16), vec(b_ple_gate), w_ple_proj.astype(BF16),
      vec(ln_g), vec(ln_b), y)


def _attention_block(x2, b, s, w_in, diff_p, nsa_p, lambda_init):
    t = b * s
    qkv, kv, gates = _input_projection(x2, w_in)
    qkv3 = qkv.reshape(b, s, C_QKV)
    kv4 = kv.reshape(N_KV, b, s, NSA_HD)
    o_diff = _diff_attention(qkv3, *diff_p, lambda_init)

    halves = lambda kind: kv[kind * NSA_KV:(kind + 1) * NSA_KV].reshape(
        NSA_KV, b, s // CMP_STRIDE, CMP_STRIDE * NSA_HD)
    pos_k, pos_v, phi_k1, phi_k2, phi_v1, phi_v2 = nsa_p
    k_cmp = _compress(halves(0), pos_k, phi_k1, phi_k2)
    v_cmp = _compress(halves(1), pos_v, phi_v1, phi_v2)
    o_cmp, qs = _cmp_select(qkv3, k_cmp, v_cmp)
    o_slc = _slc_attention(qs, kv4)
    g4 = gates[:, :NSA_GATES].reshape(b, s, NSA_HEADS, 3).transpose(0, 2, 1, 3)
    o_nsa = _win_combine(qs, kv4, o_cmp, o_slc, g4)
    return o_diff.reshape(t, DIFF_V), o_nsa.reshape(t, NSA_Q)


def _moe_block(h, top_e, gate, rank, counts, w1, b1, w2, b2):
    t = h.shape[0]
    n_blocks = (t * TOP_K + MOE_BLOCK - 1) // MOE_BLOCK + N_EXPERTS
    n_pad = n_blocks * MOE_BLOCK
    counts = counts.reshape(N_EXPERTS)
    pcounts = (counts + MOE_BLOCK - 1) // MOE_BLOCK * MOE_BLOCK
    pend = jnp.cumsum(pcounts)
    pstart = (pend - pcounts).astype(jnp.int32)
    blk_start = jnp.arange(n_blocks, dtype=jnp.int32) * MOE_BLOCK
    blk_expert = jnp.minimum(jnp.sum(pend[None, :] <= blk_start[:, None], axis=1),
                             N_EXPERTS - 1).astype(jnp.int32)
    n_used = (pend[-1:] // MOE_BLOCK).astype(jnp.int32)
    zero_flag = ((blk_start + MOE_BLOCK == pend[blk_expert])
                 | (blk_start >= pend[-1])).astype(jnp.int32)
    dest = _dest(top_e, rank, pstart).reshape(t * TOP_K)
    xs = _dispatch(h, dest, zero_flag, n_pad)
    y = _experts(xs, blk_expert, n_used, w1, b1, w2, b2)
    return dest, y


def kernel(x, p, w_in, diff_lq1, diff_lk1, diff_lq2, diff_lk2, diff_subln_g, nsa_pos_k, nsa_pos_v,
           nsa_phi_k1, nsa_phi_k2, nsa_phi_v1, nsa_phi_v2, w_br_diff, w_br_nsa, w_mgate, b_mgate, w_o,
           ln1_g, ln1_b, w_router, b_router, w_e1, b_e1, w_e2, b_e2, w_ple_gate, b_ple_gate,
           w_ple_proj, ln2_g, ln2_b):
    b, s, d = x.shape
    depth = w_in.shape[0]
    alpha = (2.0 * depth) ** 0.25
    x2 = x.reshape(b * s, d)
    for i in range(depth):
        lambda_init = 0.8 - 0.6 * math.exp(-0.3 * i)
        o_diff, o_nsa = _attention_block(
            x2, b, s, w_in[i],
            (diff_lq1[i], diff_lk1[i], diff_lq2[i], diff_lk2[i], diff_subln_g[i]),
            (nsa_pos_k[i], nsa_pos_v[i], nsa_phi_k1[i], nsa_phi_k2[i], nsa_phi_v1[i], nsa_phi_v2[i]),
            lambda_init)
        mixed = _mix(x2, o_diff, o_nsa, w_mgate[i], b_mgate[i], w_br_diff[i], w_br_nsa[i])
        h, top_e, gate, rank, counts = _post_attn(mixed, x2, w_o[i], ln1_g[i], ln1_b[i],
                                                  w_router[i], b_router[i], alpha=alpha)
        dest, y = _moe_block(h, top_e, gate, rank, counts, w_e1[i], b_e1[i], w_e2[i], b_e2[i])
        x2 = _final(h, gate, dest, y, p[i].reshape(b * s, -1), w_ple_gate[i], b_ple_gate[i],
                    w_ple_proj[i], ln2_g[i], ln2_b[i], alpha=alpha)
    return x2.reshape(b, s, d)
```

```python
import functools
import math

import numpy as np
import jax
import jax.numpy as jnp
from jax import lax
from jax.experimental import pallas as pl
from jax.experimental.pallas import tpu as pltpu

D_MODEL = 1024
DIFF_HEADS = 8
DIFF_HD = 64
NSA_HEADS = 16
NSA_KV = 4
NSA_HD = 64
HPG = NSA_HEADS // NSA_KV
CMP_LEN = 32
CMP_STRIDE = 16
SLC_LEN = 64
SLC_SHIFT = 6
SLC_TOPN = 16
WIN = 512
PHI_HID = 256
N_EXPERTS = 32
TOP_K = 4
D_FF = 1024
SWIGLU_LIMIT = 7.0
SWIGLU_ALPHA = 1.702
MOE_BLOCK = 512
PLE_DIM = 256
LN_EPS = 1e-5
NEG = -1e30
BIG = 1e30
LOWEST = -3e38
LOG2E = 1.4426950408889634

DIFF_QK = DIFF_HEADS * 2 * DIFF_HD
DIFF_V = DIFF_HEADS * 2 * DIFF_HD
NSA_Q = NSA_HEADS * NSA_HD
NSA_KVW = NSA_KV * NSA_HD
NSA_GATES = NSA_HEADS * 3
IN_SIZES = (DIFF_QK, DIFF_QK, DIFF_V, NSA_Q) + (NSA_KVW,) * 6 + (NSA_GATES,)
C_MAIN = sum(IN_SIZES[:-1])

VMEM_LIMIT = 48 * 1024 * 1024
EXPERT_VMEM_LIMIT = 56 * 1024 * 1024

BF16 = jnp.bfloat16
F32 = jnp.float32


def _alibi_slopes_log2(n):
    return (2.0 ** (-8.0 * np.arange(1, n + 1) / n) * LOG2E).astype(np.float32)


def _cparams(sem, vmem=VMEM_LIMIT):
    return pltpu.CompilerParams(dimension_semantics=sem, vmem_limit_bytes=vmem)


PROJ_CHUNK = 512
GATE_PAD = 128
C_QKV = 2 * DIFF_QK + DIFF_V + NSA_Q
N_KV = (C_MAIN - C_QKV) // NSA_HD


def _proj_kernel(x_ref, w_ref, s_ref, o_ref, kv_ref, g_ref):
    xb = x_ref[...].astype(BF16)
    for c in range(C_MAIN // PROJ_CHUNK):
        cs = slice(c * PROJ_CHUNK, (c + 1) * PROJ_CHUNK)
        acc = jnp.dot(xb, w_ref[:, cs], preferred_element_type=F32)
        if c < C_QKV // PROJ_CHUNK:
            o_ref[:, cs] = (acc * s_ref[:, cs]).astype(o_ref.dtype)
        else:
            per_chunk = PROJ_CHUNK // NSA_HD
            for j in range(per_chunk):
                stream = (c - C_QKV // PROJ_CHUNK) * per_chunk + j
                kv_ref[stream] = acc[:, j * NSA_HD:(j + 1) * NSA_HD].astype(kv_ref.dtype)
    gl = jnp.dot(xb, w_ref[:, C_MAIN:], preferred_element_type=F32)
    g_ref[...] = jax.nn.sigmoid(gl)


def _input_projection(x2, w_in, *, tm=512):
    m, k = x2.shape
    tm = min(tm, m)
    assert m % tm == 0 and C_MAIN % PROJ_CHUNK == 0 and C_QKV % PROJ_CHUNK == 0
    w = jnp.pad(w_in, ((0, 0), (0, GATE_PAD - NSA_GATES))).astype(BF16)
    scale = np.ones((1, C_QKV), np.float32)
    scale[0, :DIFF_QK] = DIFF_HD ** -0.5 * LOG2E
    scale[0, C_QKV - NSA_Q:] = NSA_HD ** -0.5 * LOG2E
    n_all = C_MAIN + GATE_PAD
    return pl.pallas_call(
        _proj_kernel,
        grid=(m // tm,),
        in_specs=[pl.BlockSpec((tm, k), lambda i: (i, 0)),
                  pl.BlockSpec((k, n_all), lambda i: (0, 0)),
                  pl.BlockSpec((1, C_QKV), lambda i: (0, 0))],
        out_specs=[pl.BlockSpec((tm, C_QKV), lambda i: (i, 0)),
                   pl.BlockSpec((N_KV, tm, NSA_HD), lambda i: (0, i, 0)),
                   pl.BlockSpec((tm, GATE_PAD), lambda i: (i, 0))],
        out_shape=[jax.ShapeDtypeStruct((m, C_QKV), BF16),
                   jax.ShapeDtypeStruct((N_KV, m, NSA_HD), BF16),
                   jax.ShapeDtypeStruct((m, GATE_PAD), F32)],
        compiler_params=_cparams(("parallel",)),
        name="input_projection",
    )(x2, w, jnp.asarray(scale))


LANES = 128


def _online_update(s, v, m_old, l_old, acc_old):
    tk = s.shape[1]
    hd = acc_old.shape[1]
    m_new = jnp.maximum(m_old, jnp.max(s, axis=-1, keepdims=True))
    alpha = jnp.exp2(m_old - m_new)
    p = jnp.exp2(s - jnp.concatenate([m_new] * (tk // LANES), axis=1))
    l_new = None
    if l_old is not None:
        psum = p[:, :LANES]
        for c in range(1, tk // LANES):
            psum = psum + p[:, c * LANES:(c + 1) * LANES]
        l_new = alpha * l_old + psum
    acc_new = alpha[:, :hd] * acc_old + jnp.dot(p.astype(BF16), v, preferred_element_type=F32)
    return m_new, l_new, acc_new


def _online_step(s, v, m_ref, l_ref, acc_ref):
    m_new, l_new, acc_new = _online_update(s, v, m_ref[...], l_ref[...], acc_ref[...])
    m_ref[...] = m_new
    l_ref[...] = l_new
    acc_ref[...] = acc_new


def _row_total(l):
    return jnp.sum(l, axis=-1, keepdims=True)


def _tri_steps(n):
    qi, ki = [], []
    for q in range(n):
        for k in range(q + 1):
            qi.append(q)
            ki.append(k)
    return np.asarray(qi, np.int32), np.asarray(ki, np.int32)


def _diff_kernel(qi_ref, ki_ref, sl_ref, q_ref, k_ref, v_ref,
                 lq1_ref, lk1_ref, lq2_ref, lk2_ref, g_ref, o_ref,
                 qm0, qm1, m0, l0, a0, m1, l1, a1, *, t, lambda_init):
    h = pl.program_id(1)
    step = pl.program_id(2)
    qt = qi_ref[step]
    kt = ki_ref[step]
    slope = sl_ref[h]

    @pl.when(kt == 0)
    def _():
        q = q_ref[0]
        lane = lax.broadcasted_iota(jnp.int32, q.shape, 1)
        qm0[...] = jnp.where(lane < DIFF_HD, q, jnp.zeros_like(q))
        qm1[...] = jnp.where(lane >= DIFF_HD, q, jnp.zeros_like(q))
        for m, l, a in ((m0, l0, a0), (m1, l1, a1)):
            m[...] = jnp.full(m.shape, NEG, F32)
            l[...] = jnp.zeros(l.shape, F32)
            a[...] = jnp.zeros(a.shape, F32)

    col = lax.broadcasted_iota(jnp.int32, (1, t), 1)
    bias = slope * ((kt - qt) * t + col).astype(F32)
    k = k_ref[0]
    v = v_ref[0]

    def scores(qm):
        return lax.dot_general(qm[...], k, (((1,), (1,)), ((), ())), preferred_element_type=F32)

    @pl.when(kt < qt)
    def _():
        _online_step(scores(qm0) + bias, v, m0, l0, a0)
        _online_step(scores(qm1) + bias, v, m1, l1, a1)

    @pl.when(kt == qt)
    def _():
        row = lax.broadcasted_iota(jnp.int32, (t, t), 0)
        colf = lax.broadcasted_iota(jnp.int32, (t, t), 1)
        bias_c = jnp.where(colf <= row, bias, NEG)
        _online_step(scores(qm0) + bias_c, v, m0, l0, a0)
        _online_step(scores(qm1) + bias_c, v, m1, l1, a1)
        lam = (jnp.exp(jnp.sum(lq1_ref[...] * lk1_ref[...], axis=-1, keepdims=True))
               - jnp.exp(jnp.sum(lq2_ref[...] * lk2_ref[...], axis=-1, keepdims=True))
               + lambda_init)
        o = a0[...] / _row_total(l0[...]) - lam * (a1[...] / _row_total(l1[...]))
        ms = jnp.mean(o * o, axis=-1, keepdims=True)
        o = o * lax.rsqrt(ms + LN_EPS) * g_ref[...] * (1.0 - lambda_init)
        o_ref[0] = o.astype(o_ref.dtype)


def _diff_attention(proj3, lq1, lk1, lq2, lk2, g, lambda_init, *, t=1024):
    b, s, _ = proj3.shape
    t = min(t, s)
    qi, ki = _tri_steps(s // t)
    w = 2 * DIFF_HD
    slopes = jnp.asarray(_alibi_slopes_log2(DIFF_HEADS))
    vec = lambda a: a.reshape(1, -1).astype(F32)
    small = lambda c: pl.BlockSpec((1, c), lambda b_, h, st, qi, ki, sl: (0, 0))
    grid_spec = pltpu.PrefetchScalarGridSpec(
        num_scalar_prefetch=3,
        grid=(b, DIFF_HEADS, len(qi)),
        in_specs=[pl.BlockSpec((1, t, w), lambda b_, h, st, qi, ki, sl: (b_, qi[st], h)),
                  pl.BlockSpec((1, t, w), lambda b_, h, st, qi, ki, sl: (b_, ki[st], DIFF_HEADS + h)),
                  pl.BlockSpec((1, t, w), lambda b_, h, st, qi, ki, sl: (b_, ki[st], 2 * DIFF_HEADS + h)),
                  small(DIFF_HD), small(DIFF_HD), small(DIFF_HD), small(DIFF_HD), small(w)],
        out_specs=pl.BlockSpec((1, t, w), lambda b_, h, st, qi, ki, sl: (b_, qi[st], h)),
        scratch_shapes=[pltpu.VMEM((t, w), BF16), pltpu.VMEM((t, w), BF16),
                        pltpu.VMEM((t, LANES), F32), pltpu.VMEM((t, LANES), F32), pltpu.VMEM((t, w), F32),
                        pltpu.VMEM((t, LANES), F32), pltpu.VMEM((t, LANES), F32), pltpu.VMEM((t, w), F32)],
    )
    return pl.pallas_call(
        functools.partial(_diff_kernel, t=t, lambda_init=lambda_init),
        grid_spec=grid_spec,
        out_shape=jax.ShapeDtypeStruct((b, s, DIFF_V), BF16),
        compiler_params=_cparams(("parallel", "parallel", "arbitrary")),
        name="diff_attention",
    )(jnp.asarray(qi), jnp.asarray(ki), slopes, proj3, proj3, proj3,
      vec(lq1), vec(lk1), vec(lq2), vec(lk2), vec(g))


def _compress_kernel(a_ref, w1_ref, w2_ref, pos_ref, o_ref):
    a = a_ref[0, 0]
    half = CMP_STRIDE * NSA_HD
    top = jnp.dot(a, w1_ref[:half, :], preferred_element_type=F32)
    bot = jnp.dot(a, w1_ref[half:, :], preferred_element_type=F32)
    pos = pos_ref[...]
    pos_hi = pos.astype(BF16)
    pos_lo = (pos - pos_hi.astype(F32)).astype(BF16)
    c = (jnp.dot(pos_hi, w1_ref[...], preferred_element_type=F32)
         + jnp.dot(pos_lo, w1_ref[...], preferred_element_type=F32))[0:1]
    n = a.shape[0]
    hid = top + pltpu.roll(bot, n - 1, axis=0) + c
    hid = hid * jax.nn.sigmoid(hid)
    o_ref[0, 0] = jnp.dot(hid.astype(BF16), w2_ref[...], preferred_element_type=F32).astype(o_ref.dtype)


def _compress(t_halves, pos, w1, w2):
    b, g, nh, _ = t_halves.shape
    pos8 = jnp.broadcast_to(pos.reshape(1, -1).astype(F32), (8, CMP_LEN * NSA_HD))
    return pl.pallas_call(
        _compress_kernel,
        grid=(b, g),
        in_specs=[pl.BlockSpec((1, 1, nh, CMP_STRIDE * NSA_HD), lambda i, j: (i, j, 0, 0)),
                  pl.BlockSpec((CMP_LEN * NSA_HD, PHI_HID), lambda i, j: (0, 0)),
                  pl.BlockSpec((PHI_HID, NSA_HD), lambda i, j: (0, 0)),
                  pl.BlockSpec((8, CMP_LEN * NSA_HD), lambda i, j: (0, 0))],
        out_specs=pl.BlockSpec((1, 1, nh, NSA_HD), lambda i, j: (i, j, 0, 0)),
        out_shape=jax.ShapeDtypeStruct((b, g, nh, NSA_HD), BF16),
        compiler_params=_cparams(("parallel", "parallel")),
        name="nsa_compress",
    )(t_halves, w1.astype(BF16), w2.astype(BF16), pos8)


SEL_LANES = 64


def _cmp_select_kernel(sl_ref, q_ref, kc_ref, vc_ref, o_ref, qs_ref, *, tq, n_cmp_pad, n_slc, n_top):
    g = pl.program_id(1)
    qt = pl.program_id(2)
    t_col = qt * tq + lax.broadcasted_iota(jnp.int32, (tq, 1), 0)
    cmp_pos = lax.broadcasted_iota(jnp.int32, (1, n_cmp_pad), 1) * CMP_STRIDE + (CMP_LEN - 1)
    mask_add = jnp.where(t_col >= cmp_pos, 0.0, NEG)
    row_ok = jnp.where(t_col >= CMP_LEN - 1, 1.0, 0.0)
    cmp_posf = (cmp_pos - qt * tq).astype(F32)
    kc = kc_ref[0, 0]
    vc = vc_ref[0, 0]
    psum = jnp.zeros((tq, n_cmp_pad), F32)
    for hh in range(HPG):
        slope = sl_ref[g * HPG + hh]
        q_h = q_ref[0][:, hh * NSA_HD:(hh + 1) * NSA_HD]
        s = lax.dot_general(q_h, kc, (((1,), (1,)), ((), ())), preferred_element_type=F32)
        s = s + (mask_add + slope * cmp_posf)
        m = jnp.max(s, axis=-1, keepdims=True)
        p = jnp.exp2(s - m)
        l = jnp.sum(p, axis=-1, keepdims=True)
        p = p * (row_ok / l)
        o_ref[0, hh] = jnp.dot(p.astype(BF16), vc, preferred_element_type=F32)
        psum = psum + p
    c_lo = lax.broadcasted_iota(jnp.int32, (n_cmp_pad, 128), 0) * CMP_STRIDE
    s_lo = lax.broadcasted_iota(jnp.int32, (n_cmp_pad, 128), 1) * SLC_LEN
    ov = jnp.minimum(c_lo + CMP_LEN, s_lo + SLC_LEN) - jnp.maximum(c_lo, s_lo)
    cmap = (jnp.maximum(ov, 0).astype(F32) * (1.0 / CMP_LEN)).astype(BF16)
    ps_hi = psum.astype(BF16)
    ps_lo = (psum - ps_hi.astype(F32)).astype(BF16)
    imp = (jnp.dot(ps_hi, cmap, preferred_element_type=F32)
           + jnp.dot(ps_lo, cmap, preferred_element_type=F32))
    imp_t = jnp.transpose(imp)[:n_slc]
    blk = lax.broadcasted_iota(jnp.int32, (n_slc, tq), 0)
    t_row = qt * tq + lax.broadcasted_iota(jnp.int32, (n_slc, tq), 1)
    cur = lax.shift_right_logical(t_row, SLC_SHIFT)
    forced = (blk == 0) | (blk == cur) | (blk == cur - 1)
    x = jnp.where(forced, BIG, jnp.where(blk <= cur, imp_t, NEG))
    sel = jnp.zeros((n_slc, tq), F32)
    for _ in range(n_top):
        mx = jnp.max(x, axis=0, keepdims=True)
        idx = jnp.min(jnp.where(x == mx, blk, n_slc), axis=0, keepdims=True)
        hit = blk == idx
        sel = jnp.where(hit, 1.0, sel)
        x = jnp.where(hit, LOWEST, x)
    if n_slc < 128:
        sel = jnp.concatenate([sel, jnp.zeros((128 - n_slc, tq), F32)], axis=0)
    sel_q = jnp.transpose(sel)[:, :SEL_LANES]
    sel_bias = jnp.where(sel_q > 0.5, 0.0, NEG)
    for hh in range(HPG):
        q_h = q_ref[0][:, hh * NSA_HD:(hh + 1) * NSA_HD]
        qs_ref[0, hh] = jnp.concatenate([q_h.astype(F32), sel_bias], axis=1).astype(qs_ref.dtype)


def _cmp_select(qkv3, k_cmp, v_cmp, *, tq=512):
    b, s, _ = qkv3.shape
    tq = min(tq, s)
    n_cmp_pad = k_cmp.shape[2]
    gw = HPG * NSA_HD
    q_blk0 = (C_QKV - NSA_Q) // gw
    n_slc = s // SLC_LEN
    n_top = min(SLC_TOPN, n_slc)
    assert n_slc <= SEL_LANES
    slopes = jnp.asarray(_alibi_slopes_log2(NSA_HEADS))
    grid_spec = pltpu.PrefetchScalarGridSpec(
        num_scalar_prefetch=1,
        grid=(b, NSA_KV, s // tq),
        in_specs=[pl.BlockSpec((1, tq, gw), lambda i, j, k, sl: (i, k, q_blk0 + j)),
                  pl.BlockSpec((1, 1, n_cmp_pad, NSA_HD), lambda i, j, k, sl: (j, i, 0, 0)),
                  pl.BlockSpec((1, 1, n_cmp_pad, NSA_HD), lambda i, j, k, sl: (j, i, 0, 0))],
        out_specs=[pl.BlockSpec((1, HPG, tq, NSA_HD), lambda i, j, k, sl: (i, j, k, 0)),
                   pl.BlockSpec((1, HPG, tq, NSA_HD + SEL_LANES), lambda i, j, k, sl: (i, j, k, 0))],
    )
    return pl.pallas_call(
        functools.partial(_cmp_select_kernel, tq=tq, n_cmp_pad=n_cmp_pad, n_slc=n_slc, n_top=n_top),
        grid_spec=grid_spec,
        out_shape=[jax.ShapeDtypeStruct((b, NSA_HEADS, s, NSA_HD), F32),
                   jax.ShapeDtypeStruct((b, NSA_HEADS, s, NSA_HD + SEL_LANES), BF16)],
        compiler_params=_cparams(("parallel", "parallel", "parallel")),
        name="nsa_cmp_select",
    )(slopes, qkv3, k_cmp, v_cmp)


def _slc_kernel(qi_ref, ki_ref, sl_ref, q_ref, k_ref, v_ref, o_ref, m_sc, a_sc, *, t):
    g = pl.program_id(1)
    step = pl.program_id(2)
    qt = qi_ref[step]
    kt = ki_ref[step]

    @pl.when(kt == 0)
    def _():
        m_sc[...] = jnp.full(m_sc.shape, NEG, F32)
        a_sc[...] = jnp.zeros(a_sc.shape, F32)

    col = lax.broadcasted_iota(jnp.int32, (1, t), 1)
    rel = ((kt - qt) * t + col).astype(F32)
    blk_of_key = lax.shift_right_logical(kt * t + lax.broadcasted_iota(jnp.int32, (t, SEL_LANES), 0), SLC_SHIFT)
    onehot = jnp.where(blk_of_key == lax.broadcasted_iota(jnp.int32, (t, SEL_LANES), 1), 1.0, 0.0)
    k = jnp.concatenate([k_ref[0, 0].astype(F32), onehot], axis=1).astype(BF16)
    ones_col = jnp.where(lax.broadcasted_iota(jnp.int32, (t, NSA_HD), 1) == 0, 1.0, 0.0)
    v = jnp.concatenate([v_ref[0, 0].astype(F32), ones_col], axis=1).astype(BF16)

    def sweep(mask_add):
        for hh in range(HPG):
            s = lax.dot_general(q_ref[0, hh], k, (((1,), (1,)), ((), ())), preferred_element_type=F32)
            bias = sl_ref[g * HPG + hh] * rel
            s = s + (bias if mask_add is None else mask_add + bias)
            m_new, _, a_new = _online_update(s, v, m_sc[hh], None, a_sc[hh])
            m_sc[hh] = m_new
            a_sc[hh] = a_new

    @pl.when(kt < qt)
    def _():
        sweep(None)

    @pl.when(kt == qt)
    def _():
        row = lax.broadcasted_iota(jnp.int32, (t, t), 0)
        colf = lax.broadcasted_iota(jnp.int32, (t, t), 1)
        sweep(jnp.where(colf <= row, 0.0, NEG))
        for hh in range(HPG):
            acc = a_sc[hh]
            o_ref[0, hh] = acc[:, :NSA_HD] / acc[:, NSA_HD:NSA_HD + 1]


def _slc_attention(qs, kv4, *, t=1024):
    b, _, s, w = qs.shape
    t = min(t, s)
    qi, ki = _tri_steps(s // t)
    slopes = jnp.asarray(_alibi_slopes_log2(NSA_HEADS))
    grid_spec = pltpu.PrefetchScalarGridSpec(
        num_scalar_prefetch=3,
        grid=(b, NSA_KV, len(qi)),
        in_specs=[pl.BlockSpec((1, HPG, t, w), lambda i, g, st, qi, ki, sl: (i, g, qi[st], 0)),
                  pl.BlockSpec((1, 1, t, NSA_HD), lambda i, g, st, qi, ki, sl: (2 * NSA_KV + g, i, ki[st], 0)),
                  pl.BlockSpec((1, 1, t, NSA_HD), lambda i, g, st, qi, ki, sl: (3 * NSA_KV + g, i, ki[st], 0))],
        out_specs=pl.BlockSpec((1, HPG, t, NSA_HD), lambda i, g, st, qi, ki, sl: (i, g, qi[st], 0)),
        scratch_shapes=[pltpu.VMEM((HPG, t, LANES), F32), pltpu.VMEM((HPG, t, 2 * NSA_HD), F32)],
    )
    return pl.pallas_call(
        functools.partial(_slc_kernel, t=t),
        grid_spec=grid_spec,
        out_shape=jax.ShapeDtypeStruct((b, NSA_HEADS, s, NSA_HD), F32),
        compiler_params=_cparams(("parallel", "parallel", "arbitrary")),
        name="nsa_slc_attention",
    )(jnp.asarray(qi), jnp.asarray(ki), slopes, qs, kv4, kv4)


def _win_kernel(sl_ref, q_ref, *rest, t):
    nblk = WIN // t
    k_refs = rest[:nblk + 1]
    v_refs = rest[nblk + 1:2 * nblk + 2]
    oc_ref, os_ref, g_ref, o_ref = rest[2 * nblk + 2:]
    g = pl.program_id(1)
    qt = pl.program_id(2)
    row = lax.broadcasted_iota(jnp.int32, (t, t), 0)
    col = lax.broadcasted_iota(jnp.int32, (t, t), 1)
    colr = lax.broadcasted_iota(jnp.int32, (1, t), 1)
    refs = tuple(zip(k_refs, v_refs))
    masks = []
    for j in range(len(refs)):
        off = j - nblk
        if off == -nblk:
            masks.append(jnp.where((col > row) & (qt + off >= 0), 0.0, NEG))
        elif off < 0:
            masks.append(jnp.where(qt + off >= 0, 0.0, NEG))
        else:
            masks.append(jnp.where(col <= row, 0.0, NEG))
    outs = []
    for hh in range(HPG):
        slope = sl_ref[g * HPG + hh]
        q = q_ref[0, hh][:, :NSA_HD]
        m = jnp.full((t, LANES), NEG, F32)
        l = jnp.zeros((t, LANES), F32)
        acc = jnp.zeros((t, NSA_HD), F32)
        for j, (k_ref, v_ref) in enumerate(refs):
            off = j - nblk
            s = lax.dot_general(q, k_ref[0, 0], (((1,), (1,)), ((), ())), preferred_element_type=F32)
            s = s + (masks[j] + slope * (off * t + colr).astype(F32))
            m, l, acc = _online_update(s, v_ref[0, 0], m, l, acc)
        o_win = acc / _row_total(l)
        gt = g_ref[0, hh]
        outs.append(gt[:, 0:1] * oc_ref[0, hh] + gt[:, 1:2] * os_ref[0, hh] + gt[:, 2:3] * o_win)
    o_ref[0] = jnp.concatenate(outs, axis=1).astype(o_ref.dtype)


def _win_combine(qs, kv4, o_cmp, o_slc, gates, *, t=512):
    b, _, s, w = qs.shape
    t = min(t, s)
    assert WIN % t == 0
    nblk = WIN // t
    slopes = jnp.asarray(_alibi_slopes_log2(NSA_HEADS))
    kv = lambda kind, off: pl.BlockSpec(
        (1, 1, t, NSA_HD), lambda i, g, qt, sl: (kind * NSA_KV + g, i, jnp.maximum(qt + off, 0), 0))
    per_head = lambda c: pl.BlockSpec((1, HPG, t, c), lambda i, g, qt, sl: (i, g, qt, 0))
    grid_spec = pltpu.PrefetchScalarGridSpec(
        num_scalar_prefetch=1,
        grid=(b, NSA_KV, s // t),
        in_specs=([per_head(w)] + [kv(4, off) for off in range(-nblk, 1)]
                  + [kv(5, off) for off in range(-nblk, 1)]
                  + [per_head(NSA_HD), per_head(NSA_HD), per_head(3)]),
        out_specs=pl.BlockSpec((1, t, HPG * NSA_HD), lambda i, g, qt, sl: (i, qt, g)),
    )
    return pl.pallas_call(
        functools.partial(_win_kernel, t=t),
        grid_spec=grid_spec,
        out_shape=jax.ShapeDtypeStruct((b, s, NSA_Q), BF16),
        compiler_params=_cparams(("parallel", "parallel", "parallel")),
        name="nsa_win_combine",
    )(slopes, qs, *([kv4] * (2 * nblk + 2)), o_cmp, o_slc, gates)


def _mix_kernel(x_ref, od_ref, on_ref, wg1_ref, wg2_ref, bg1_ref, bg2_ref, wd_ref, wn_ref, o_ref, xb_ref):
    @pl.when(pl.program_id(1) == 0)
    def _():
        xb_ref[...] = x_ref[...].astype(BF16)

    xb = xb_ref[...]
    g1 = jax.nn.sigmoid(jnp.dot(xb, wg1_ref[...], preferred_element_type=F32) + bg1_ref[...])
    g2 = jax.nn.sigmoid(jnp.dot(xb, wg2_ref[...], preferred_element_type=F32) + bg2_ref[...])
    bd = jnp.dot(od_ref[...], wd_ref[...], preferred_element_type=F32)
    bn = jnp.dot(on_ref[...], wn_ref[...], preferred_element_type=F32)
    o_ref[...] = (g1 * bd + g2 * bn).astype(o_ref.dtype)


def _mix(x2, o_diff, o_nsa, w_mgate, b_mgate, w_br_diff, w_br_nsa, *, tm=512, tn=512):
    m, d = x2.shape
    tm = min(tm, m)
    nj = d // tn
    wg = w_mgate.astype(BF16)
    bg = b_mgate.reshape(1, -1).astype(F32)
    row = pl.BlockSpec((tm, d), lambda i, j: (i, 0))
    return pl.pallas_call(
        _mix_kernel,
        grid=(m // tm, nj),
        in_specs=[row, row, row,
                  pl.BlockSpec((d, tn), lambda i, j: (0, j)),
                  pl.BlockSpec((d, tn), lambda i, j: (0, j + nj)),
                  pl.BlockSpec((1, tn), lambda i, j: (0, j)),
                  pl.BlockSpec((1, tn), lambda i, j: (0, j + nj)),
                  pl.BlockSpec((d, tn), lambda i, j: (0, j)),
                  pl.BlockSpec((d, tn), lambda i, j: (0, j))],
        out_specs=pl.BlockSpec((tm, tn), lambda i, j: (i, j)),
        out_shape=jax.ShapeDtypeStruct((m, d), BF16),
        scratch_shapes=[pltpu.VMEM((tm, d), BF16)],
        compiler_params=_cparams(("parallel", "arbitrary")),
        name="branch_mix",
    )(x2, o_diff, o_nsa, wg, wg, bg, bg, w_br_diff.astype(BF16), w_br_nsa.astype(BF16))


def _layer_norm_rows(y, g, b):
    mu = jnp.mean(y, axis=-1, keepdims=True)
    yc = y - mu
    var = jnp.mean(yc * yc, axis=-1, keepdims=True)
    return yc * lax.rsqrt(var + LN_EPS) * g + b


def _split_bf16(a):
    hi = a.astype(BF16)
    lo = (a - hi.astype(F32)).astype(BF16)
    return hi, lo


def _post_attn_kernel(mx_ref, x_ref, wo_ref, g_ref, b_ref, wr_hi_ref, wr_lo_ref, br_ref,
                      h_ref, e_ref, gate_ref, rank_ref, cnt_ref, run_ref, *, tm, alpha):
    i = pl.program_id(0)

    @pl.when(i == 0)
    def _():
        run_ref[...] = jnp.zeros(run_ref.shape, F32)

    y = alpha * x_ref[...] + jnp.dot(mx_ref[...], wo_ref[...], preferred_element_type=F32)
    h = _layer_norm_rows(y, g_ref[...], b_ref[...])
    h_ref[...] = h
    h_hi, h_lo = _split_bf16(h)
    logits = (jnp.dot(h_hi, wr_hi_ref[...], preferred_element_type=F32)
              + jnp.dot(h_hi, wr_lo_ref[...], preferred_element_type=F32)
              + jnp.dot(h_lo, wr_hi_ref[...], preferred_element_type=F32)) + br_ref[...]
    lane = lax.broadcasted_iota(jnp.int32, (tm, N_EXPERTS), 1)
    lane4 = lax.broadcasted_iota(jnp.int32, (tm, TOP_K), 1)
    xl = logits
    vals, idxs, hits = [], [], []
    for _ in range(TOP_K):
        mx = jnp.max(xl, axis=-1, keepdims=True)
        idx = jnp.min(jnp.where(xl == mx, lane, N_EXPERTS), axis=-1, keepdims=True)
        hit = lane == idx
        vals.append(mx)
        idxs.append(idx)
        hits.append(hit)
        xl = jnp.where(hit, LOWEST, xl)
    ex = [jnp.exp(v - vals[0]) for v in vals]
    den = ex[0] + ex[1] + ex[2] + ex[3]
    member = jnp.zeros((tm, N_EXPERTS), F32)
    for hit in hits:
        member = jnp.where(hit, 1.0, member)
    r_i = lax.broadcasted_iota(jnp.int32, (tm, tm), 0)
    c_i = lax.broadcasted_iota(jnp.int32, (tm, tm), 1)
    tri = jnp.where(c_i < r_i, 1.0, 0.0).astype(BF16)
    before = jnp.dot(tri, member.astype(BF16), preferred_element_type=F32) + run_ref[...]
    e_out = jnp.zeros((tm, TOP_K), jnp.int32)
    g_out = jnp.zeros((tm, TOP_K), F32)
    r_out = jnp.zeros((tm, TOP_K), jnp.int32)
    for k in range(TOP_K):
        rk = jnp.sum(jnp.where(hits[k], before, 0.0), axis=-1, keepdims=True)
        e_out = jnp.where(lane4 == k, idxs[k], e_out)
        g_out = jnp.where(lane4 == k, ex[k] / den, g_out)
        r_out = jnp.where(lane4 == k, rk.astype(jnp.int32), r_out)
    e_ref[...] = e_out
    gate_ref[...] = g_out
    rank_ref[...] = r_out
    run_ref[...] = run_ref[...] + jnp.sum(member, axis=0, keepdims=True)
    cnt_ref[...] = run_ref[...].astype(jnp.int32)


def _post_attn(mixed, x2, w_o, ln_g, ln_b, w_router, b_router, *, alpha, tm=256):
    m, d = x2.shape
    tm = min(tm, m)
    wr_hi, wr_lo = _split_bf16(w_router.astype(F32))
    vec = lambda a: a.reshape(1, -1).astype(F32)
    full = lambda r, c: pl.BlockSpec((r, c), lambda i: (0, 0))
    rows = lambda c: pl.BlockSpec((tm, c), lambda i: (i, 0))
    return pl.pallas_call(
        functools.partial(_post_attn_kernel, tm=tm, alpha=alpha),
        grid=(m // tm,),
        in_specs=[rows(d), rows(d), full(d, d), full(1, d), full(1, d),
                  full(d, N_EXPERTS), full(d, N_EXPERTS), full(1, N_EXPERTS)],
        out_specs=[rows(d), rows(TOP_K), rows(TOP_K), rows(TOP_K), full(1, N_EXPERTS)],
        out_shape=[jax.ShapeDtypeStruct((m, d), F32),
                   jax.ShapeDtypeStruct((m, TOP_K), jnp.int32),
                   jax.ShapeDtypeStruct((m, TOP_K), F32),
                   jax.ShapeDtypeStruct((m, TOP_K), jnp.int32),
                   jax.ShapeDtypeStruct((1, N_EXPERTS), jnp.int32)],
        scratch_shapes=[pltpu.VMEM((1, N_EXPERTS), F32)],
        compiler_params=_cparams(("arbitrary",)),
        name="post_attn_router",
    )(mixed, x2, w_o.astype(BF16), vec(ln_g), vec(ln_b), wr_hi, wr_lo, vec(b_router))


def _dest_kernel(ps_ref, e_ref, r_ref, o_ref):
    e = e_ref[...]
    base = jnp.zeros(e.shape, jnp.int32)
    for j in range(N_EXPERTS):
        base = jnp.where(e == j, ps_ref[j], base)
    o_ref[...] = base + r_ref[...]


def _dest(top_e, rank, pstart, *, tm=2048):
    m = top_e.shape[0]
    tm = min(tm, m)
    grid_spec = pltpu.PrefetchScalarGridSpec(
        num_scalar_prefetch=1,
        grid=(m // tm,),
        in_specs=[pl.BlockSpec((tm, TOP_K), lambda i, ps: (i, 0)),
                  pl.BlockSpec((tm, TOP_K), lambda i, ps: (i, 0))],
        out_specs=pl.BlockSpec((tm, TOP_K), lambda i, ps: (i, 0)),
    )
    return pl.pallas_call(
        _dest_kernel, grid_spec=grid_spec,
        out_shape=jax.ShapeDtypeStruct((m, TOP_K), jnp.int32),
        compiler_params=_cparams(("parallel",)),
        name="moe_dest",
    )(pstart, top_e, rank)


N_DMA_PRIORITIES = 2


def _dispatch_kernel(zf_ref, dest_ref, h_ref, xs_ref, zeros, sem, zsem, *, tm, n_blocks):
    @pl.when(pl.program_id(0) == 0)
    def _():
        zeros[...] = jnp.zeros(zeros.shape, zeros.dtype)

        def zero_copy(blk):
            start = pl.multiple_of(blk * MOE_BLOCK, MOE_BLOCK)
            return pltpu.make_async_copy(zeros, xs_ref.at[pl.ds(start, MOE_BLOCK), :], zsem)

        def zstart(blk, carry):
            @pl.when(zf_ref[blk] > 0)
            def _():
                zero_copy(blk).start()
            return carry

        def zwait(blk, carry):
            @pl.when(zf_ref[blk] > 0)
            def _():
                zero_copy(blk).wait()
            return carry

        lax.fori_loop(0, n_blocks, zstart, 0)
        lax.fori_loop(0, n_blocks, zwait, 0)

    def row_copy(r, k):
        return pltpu.make_async_copy(h_ref.at[pl.ds(r, 1), :],
                                     xs_ref.at[pl.ds(dest_ref[r * TOP_K + k], 1), :], sem)

    for r in range(tm):
        for k in range(TOP_K):
            row_copy(r, k).start(priority=k % N_DMA_PRIORITIES)
    for r in range(tm):
        for k in range(TOP_K):
            row_copy(r, k).wait()


def _dispatch(h, dest_flat, zero_flag, n_pad, *, tm=256):
    m, d = h.shape
    tm = min(tm, m)
    assert m % tm == 0
    n_blocks = n_pad // MOE_BLOCK
    grid_spec = pltpu.PrefetchScalarGridSpec(
        num_scalar_prefetch=1,
        grid=(m // tm,),
        in_specs=[pl.BlockSpec((tm * TOP_K,), lambda i, zf: (i,), memory_space=pltpu.SMEM),
                  pl.BlockSpec((tm, d), lambda i, zf: (i, 0))],
        out_specs=pl.BlockSpec(memory_space=pl.ANY),
        scratch_shapes=[pltpu.VMEM((MOE_BLOCK, d), h.dtype), pltpu.SemaphoreType.DMA(()),
                        pltpu.SemaphoreType.DMA(())],
    )
    return pl.pallas_call(
        functools.partial(_dispatch_kernel, tm=tm, n_blocks=n_blocks),
        grid_spec=grid_spec,
        out_shape=jax.ShapeDtypeStruct((n_pad, d), h.dtype),
        compiler_params=_cparams(("arbitrary",)),
        name="moe_dispatch",
    )(zero_flag, dest_flat, h)


def _expert_kernel(be_ref, nu_ref, xs_ref, w1_ref, b1_ref, w2_ref, b2_ref, y_ref, w1b, w2b):
    i = pl.program_id(0)
    new_expert = (i == 0) | (be_ref[i] != be_ref[jnp.maximum(i - 1, 0)])

    @pl.when((i < nu_ref[0]) & new_expert)
    def _():
        w1b[...] = w1_ref[0].astype(BF16)
        w2b[...] = w2_ref[0].astype(BF16)

    @pl.when(i < nu_ref[0])
    def _():
        xb = xs_ref[...].astype(BF16)
        hb = jnp.dot(xb, w1b[...], preferred_element_type=F32) + b1_ref[0]
        glu = jnp.minimum(hb[:, :D_FF], SWIGLU_LIMIT)
        lin = jnp.clip(hb[:, D_FF:], -SWIGLU_LIMIT, SWIGLU_LIMIT)
        act = glu * jax.nn.sigmoid(SWIGLU_ALPHA * glu) * (lin + 1.0)
        y_ref[...] = jnp.dot(act.astype(BF16), w2b[...], preferred_element_type=F32) + b2_ref[0]

    @pl.when(i >= nu_ref[0])
    def _():
        y_ref[...] = jnp.zeros(y_ref.shape, y_ref.dtype)


def _experts(xs, blk_expert, n_used, w1, b1, w2, b2):
    n_pad, d = xs.shape
    n_blocks = n_pad // MOE_BLOCK
    last = lambda i, nu: jnp.minimum(i, nu[0] - 1)
    grid_spec = pltpu.PrefetchScalarGridSpec(
        num_scalar_prefetch=2,
        grid=(n_blocks,),
        in_specs=[pl.BlockSpec((MOE_BLOCK, d), lambda i, be, nu: (last(i, nu), 0)),
                  pl.BlockSpec((1, d, 2 * D_FF), lambda i, be, nu: (be[last(i, nu)], 0, 0)),
                  pl.BlockSpec((1, 1, 2 * D_FF), lambda i, be, nu: (be[last(i, nu)], 0, 0)),
                  pl.BlockSpec((1, D_FF, d), lambda i, be, nu: (be[last(i, nu)], 0, 0)),
                  pl.BlockSpec((1, 1, d), lambda i, be, nu: (be[last(i, nu)], 0, 0))],
        out_specs=pl.BlockSpec((MOE_BLOCK, d), lambda i, be, nu: (i, 0)),
        scratch_shapes=[pltpu.VMEM((d, 2 * D_FF), BF16), pltpu.VMEM((D_FF, d), BF16)],
    )
    return pl.pallas_call(
        _expert_kernel, grid_spec=grid_spec,
        out_shape=jax.ShapeDtypeStruct((n_pad, d), F32),
        compiler_params=_cparams(("arbitrary",), vmem=EXPERT_VMEM_LIMIT),
        name="moe_experts",
    )(blk_expert, n_used, xs, w1.astype(F32), b1.reshape(N_EXPERTS, 1, -1).astype(F32),
      w2.astype(F32), b2.reshape(N_EXPERTS, 1, -1).astype(F32))


def _final_kernel(dest_ref, h_ref, gate_ref, p_ref, wpg_ref, bpg_ref, wpp_ref, g_ref, b_ref, y_ref,
                  o_ref, buf, sem, *, tm, alpha):
    def row_copy(r, k):
        return pltpu.make_async_copy(y_ref.at[pl.ds(dest_ref[r * TOP_K + k], 1), :],
                                     buf.at[k, pl.ds(r, 1), :], sem)

    for r in range(tm):
        for k in range(TOP_K):
            row_copy(r, k).start(priority=k % N_DMA_PRIORITIES)
    h = h_ref[...]
    ple = (jax.nn.sigmoid(jnp.dot(h.astype(BF16), wpg_ref[...], preferred_element_type=F32) + bpg_ref[...])
           * jnp.dot(p_ref[...].astype(BF16), wpp_ref[...], preferred_element_type=F32))
    for r in range(tm):
        for k in range(TOP_K):
            row_copy(r, k).wait()
    gate = gate_ref[...]
    ffn = gate[:, 0:1] * buf[0]
    for k in range(1, TOP_K):
        ffn = ffn + gate[:, k:k + 1] * buf[k]
    o_ref[...] = _layer_norm_rows(alpha * h + ffn + ple, g_ref[...], b_ref[...])


def _final(h, gate, dest_flat, y, p2, w_ple_gate, b_ple_gate, w_ple_proj, ln_g, ln_b, *, alpha, tm=256):
    m, d = h.shape
    tm = min(tm, m)
    vec = lambda a: a.reshape(1, -1).astype(F32)
    full = lambda r, c: pl.BlockSpec((r, c), lambda i: (0, 0))
    rows = lambda c: pl.BlockSpec((tm, c), lambda i: (i, 0))
    return pl.pallas_call(
        functools.partial(_final_kernel, tm=tm, alpha=alpha),
        grid=(m // tm,),
        in_specs=[pl.BlockSpec((tm * TOP_K,), lambda i: (i,), memory_space=pltpu.SMEM),
                  rows(d), rows(TOP_K), rows(PLE_DIM), full(d, d), full(1, d), full(PLE_DIM, d),
                  full(1, d), full(1, d), pl.BlockSpec(memory_space=pl.ANY)],
        out_specs=rows(d),
        out_shape=jax.ShapeDtypeStruct((m, d), F32),
        scratch_shapes=[pltpu.VMEM((TOP_K, tm, d), F32), pltpu.SemaphoreType.DMA(())],
        compiler_params=_cparams(("arbitrary",)),
        name="moe_combine_final",
    )(dest_flat, h, gate, p2, w_ple_gate.astype(BF16), vec(b_ple_gate), w_ple_proj.astype(BF16),
      vec(ln_g), vec(ln_b), y)


def _attention_block(x2, b, s, w_in, diff_p, nsa_p, lambda_init):
    t = b * s
    qkv, kv, gates = _input_projection(x2, w_in)
    qkv3 = qkv.reshape(b, s, C_QKV)
    kv4 = kv.reshape(N_KV, b, s, NSA_HD)
    o_diff = _diff_attention(qkv3, *diff_p, lambda_init)

    halves = lambda kind: kv[kind * NSA_KV:(kind + 1) * NSA_KV].reshape(
        NSA_KV, b, s // CMP_STRIDE, CMP_STRIDE * NSA_HD)
    pos_k, pos_v, phi_k1, phi_k2, phi_v1, phi_v2 = nsa_p
    k_cmp = _compress(halves(0), pos_k, phi_k1, phi_k2)
    v_cmp = _compress(halves(1), pos_v, phi_v1, phi_v2)
    o_cmp, qs = _cmp_select(qkv3, k_cmp, v_cmp)
    o_slc = _slc_attention(qs, kv4)
    g4 = gates[:, :NSA_GATES].reshape(b, s, NSA_HEADS, 3).transpose(0, 2, 1, 3)
    o_nsa = _win_combine(qs, kv4, o_cmp, o_slc, g4)
    return o_diff.reshape(t, DIFF_V), o_nsa.reshape(t, NSA_Q)


def _moe_block(h, top_e, gate, rank, counts, w1, b1, w2, b2):
    t = h.shape[0]
    n_blocks = (t * TOP_K + MOE_BLOCK - 1) // MOE_BLOCK + N_EXPERTS
    n_pad = n_blocks * MOE_BLOCK
    counts = counts.reshape(N_EXPERTS)
    pcounts = (counts + MOE_BLOCK - 1) // MOE_BLOCK * MOE_BLOCK
    pend = jnp.cumsum(pcounts)
    pstart = (pend - pcounts).astype(jnp.int32)
    blk_start = jnp.arange(n_blocks, dtype=jnp.int32) * MOE_BLOCK
    blk_expert = jnp.minimum(jnp.sum(pend[None, :] <= blk_start[:, None], axis=1),
                             N_EXPERTS - 1).astype(jnp.int32)
    n_used = (pend[-1:] // MOE_BLOCK).astype(jnp.int32)
    zero_flag = ((blk_start + MOE_BLOCK == pend[blk_expert])
                 | (blk_start >= pend[-1])).astype(jnp.int32)
    dest = _dest(top_e, rank, pstart).reshape(t * TOP_K)
    xs = _dispatch(h, dest, zero_flag, n_pad)
    y = _experts(xs, blk_expert, n_used, w1, b1, w2, b2)
    return dest, y


def kernel(x, p, w_in, diff_lq1, diff_lk1, diff_lq2, diff_lk2, diff_subln_g, nsa_pos_k, nsa_pos_v,
           nsa_phi_k1, nsa_phi_k2, nsa_phi_v1, nsa_phi_v2, w_br_diff, w_br_nsa, w_mgate, b_mgate, w_o,
           ln1_g, ln1_b, w_router, b_router, w_e1, b_e1, w_e2, b_e2, w_ple_gate, b_ple_gate,
           w_ple_proj, ln2_g, ln2_b):
    b, s, d = x.shape
    depth = w_in.shape[0]
    alpha = (2.0 * depth) ** 0.25
    x2 = x.reshape(b * s, d)
    for i in range(depth):
        lambda_init = 0.8 - 0.6 * math.exp(-0.3 * i)
        o_diff, o_nsa = _attention_block(
            x2, b, s, w_in[i],
            (diff_lq1[i], diff_lk1[i], diff_lq2[i], diff_lk2[i], diff_subln_g[i]),
            (nsa_pos_k[i], nsa_pos_v[i], nsa_phi_k1[i], nsa_phi_k2[i], nsa_phi_v1[i], nsa_phi_v2[i]),
            lambda_init)
        mixed = _mix(x2, o_diff, o_nsa, w_mgate[i], b_mgate[i], w_br_diff[i], w_br_nsa[i])
        h, top_e, gate, rank, counts = _post_attn(mixed, x2, w_o[i], ln1_g[i], ln1_b[i],
                                                  w_router[i], b_router[i], alpha=alpha)
        dest, y = _moe_block(h, top_e, gate, rank, counts, w_e1[i], b_e1[i], w_e2[i], b_e2[i])
        x2 = _final(h, gate, dest, y, p[i].reshape(b * s, -1), w_ple_gate[i], b_ple_gate[i],
                    w_ple_proj[i], ln2_g[i], ln2_b[i], alpha=alpha)
    return x2.reshape(b, s, d)
```

```python
import functools
import math

import numpy as np
import jax
import jax.numpy as jnp
from jax import lax
from jax.experimental import pallas as pl
from jax.experimental.pallas import tpu as pltpu

D_MODEL = 1024
DIFF_HEADS = 8
DIFF_HD = 64
NSA_HEADS = 16
NSA_KV = 4
NSA_HD = 64
HPG = NSA_HEADS // NSA_KV
CMP_LEN = 32
CMP_STRIDE = 16
SLC_LEN = 64
SLC_SHIFT = 6
SLC_TOPN = 16
WIN = 512
PHI_HID = 256
N_EXPERTS = 32
TOP_K = 4
D_FF = 1024
SWIGLU_LIMIT = 7.0
SWIGLU_ALPHA = 1.702
MOE_BLOCK = 512
PLE_DIM = 256
LN_EPS = 1e-5
NEG = -1e30
BIG = 1e30
LOWEST = -3e38
LOG2E = 1.4426950408889634

DIFF_QK = DIFF_HEADS * 2 * DIFF_HD
DIFF_V = DIFF_HEADS * 2 * DIFF_HD
NSA_Q = NSA_HEADS * NSA_HD
NSA_KVW = NSA_KV * NSA_HD
NSA_GATES = NSA_HEADS * 3
IN_SIZES = (DIFF_QK, DIFF_QK, DIFF_V, NSA_Q) + (NSA_KVW,) * 6 + (NSA_GATES,)
C_MAIN = sum(IN_SIZES[:-1])

VMEM_LIMIT = 48 * 1024 * 1024
EXPERT_VMEM_LIMIT = 56 * 1024 * 1024

BF16 = jnp.bfloat16
F32 = jnp.float32


def _alibi_slopes_log2(n):
    return (2.0 ** (-8.0 * np.arange(1, n + 1) / n) * LOG2E).astype(np.float32)


def _cparams(sem, vmem=VMEM_LIMIT):
    return pltpu.CompilerParams(dimension_semantics=sem, vmem_limit_bytes=vmem)


PROJ_CHUNK = 512
GATE_PAD = 128
C_QKV = 2 * DIFF_QK + DIFF_V + NSA_Q
N_KV = (C_MAIN - C_QKV) // NSA_HD


def _proj_kernel(x_ref, w_ref, s_ref, o_ref, kv_ref, g_ref):
    xb = x_ref[...].astype(BF16)
    for c in range(C_MAIN // PROJ_CHUNK):
        cs = slice(c * PROJ_CHUNK, (c + 1) * PROJ_CHUNK)
        acc = jnp.dot(xb, w_ref[:, cs], preferred_element_type=F32)
        if c < C_QKV // PROJ_CHUNK:
            o_ref[:, cs] = (acc * s_ref[:, cs]).astype(o_ref.dtype)
        else:
            per_chunk = PROJ_CHUNK // NSA_HD
            for j in range(per_chunk):
                stream = (c - C_QKV // PROJ_CHUNK) * per_chunk + j
                kv_ref[stream] = acc[:, j * NSA_HD:(j + 1) * NSA_HD].astype(kv_ref.dtype)
    gl = jnp.dot(xb, w_ref[:, C_MAIN:], preferred_element_type=F32)
    g_ref[...] = jax.nn.sigmoid(gl)


def _input_projection(x2, w_in, *, tm=512):
    m, k = x2.shape
    tm = min(tm, m)
    assert m % tm == 0 and C_MAIN % PROJ_CHUNK == 0 and C_QKV % PROJ_CHUNK == 0
    w = jnp.pad(w_in, ((0, 0), (0, GATE_PAD - NSA_GATES))).astype(BF16)
    scale = np.ones((1, C_QKV), np.float32)
    scale[0, :DIFF_QK] = DIFF_HD ** -0.5 * LOG2E
    scale[0, C_QKV - NSA_Q:] = NSA_HD ** -0.5 * LOG2E
    n_all = C_MAIN + GATE_PAD
    return pl.pallas_call(
        _proj_kernel,
        grid=(m // tm,),
        in_specs=[pl.BlockSpec((tm, k), lambda i: (i, 0)),
                  pl.BlockSpec((k, n_all), lambda i: (0, 0)),
                  pl.BlockSpec((1, C_QKV), lambda i: (0, 0))],
        out_specs=[pl.BlockSpec((tm, C_QKV), lambda i: (i, 0)),
                   pl.BlockSpec((N_KV, tm, NSA_HD), lambda i: (0, i, 0)),
                   pl.BlockSpec((tm, GATE_PAD), lambda i: (i, 0))],
        out_shape=[jax.ShapeDtypeStruct((m, C_QKV), BF16),
                   jax.ShapeDtypeStruct((N_KV, m, NSA_HD), BF16),
                   jax.ShapeDtypeStruct((m, GATE_PAD), F32)],
        compiler_params=_cparams(("parallel",)),
        name="input_projection",
    )(x2, w, jnp.asarray(scale))


LANES = 128


def _online_update(s, v, m_old, l_old, acc_old):
    tk = s.shape[1]
    hd = acc_old.shape[1]
    m_new = jnp.maximum(m_old, jnp.max(s, axis=-1, keepdims=True))
    alpha = jnp.exp2(m_old - m_new)
    p = jnp.exp2(s - jnp.concatenate([m_new] * (tk // LANES), axis=1))
    l_new = None
    if l_old is not None:
        psum = p[:, :LANES]
        for c in range(1, tk // LANES):
            psum = psum + p[:, c * LANES:(c + 1) * LANES]
        l_new = alpha * l_old + psum
    acc_new = alpha[:, :hd] * acc_old + jnp.dot(p.astype(BF16), v, preferred_element_type=F32)
    return m_new, l_new, acc_new


def _row_total(l):
    return jnp.sum(l, axis=-1, keepdims=True)


def _tri_steps(n):
    qi, ki = [], []
    for q in range(n):
        for k in range(q + 1):
            qi.append(q)
            ki.append(k)
    return np.asarray(qi, np.int32), np.asarray(ki, np.int32)


DIFF_HPS = 2


def _diff_kernel(qi_ref, ki_ref, sl_ref, q_ref, k_ref, v_ref,
                 lq1_ref, lk1_ref, lq2_ref, lk2_ref, g_ref, o_ref,
                 qm_sc, m_sc, l_sc, a_sc, *, t, lambda_init):
    hp = pl.program_id(1)
    step = pl.program_id(2)
    qt = qi_ref[step]
    kt = ki_ref[step]
    w = 2 * DIFF_HD

    @pl.when(kt == 0)
    def _():
        for hh in range(DIFF_HPS):
            q = q_ref[0][:, hh * w:(hh + 1) * w]
            lane = lax.broadcasted_iota(jnp.int32, q.shape, 1)
            qm_sc[2 * hh] = jnp.where(lane < DIFF_HD, q, jnp.zeros_like(q))
            qm_sc[2 * hh + 1] = jnp.where(lane >= DIFF_HD, q, jnp.zeros_like(q))
        m_sc[...] = jnp.full(m_sc.shape, NEG, F32)
        l_sc[...] = jnp.zeros(l_sc.shape, F32)
        a_sc[...] = jnp.zeros(a_sc.shape, F32)

    col = lax.broadcasted_iota(jnp.int32, (1, t), 1)
    rel = ((kt - qt) * t + col).astype(F32)

    def sweep(mask_add):
        for hh in range(DIFF_HPS):
            bias = sl_ref[hp * DIFF_HPS + hh] * rel
            bias = bias if mask_add is None else mask_add + bias
            k = k_ref[0][:, hh * w:(hh + 1) * w]
            v = v_ref[0][:, hh * w:(hh + 1) * w]
            for c in range(2):
                i = 2 * hh + c
                s = lax.dot_general(qm_sc[i], k, (((1,), (1,)), ((), ())), preferred_element_type=F32) + bias
                m_new, l_new, a_new = _online_update(s, v, m_sc[i], l_sc[i], a_sc[i])
                m_sc[i] = m_new
                l_sc[i] = l_new
                a_sc[i] = a_new

    @pl.when(kt < qt)
    def _():
        sweep(None)

    @pl.when(kt == qt)
    def _():
        row = lax.broadcasted_iota(jnp.int32, (t, t), 0)
        colf = lax.broadcasted_iota(jnp.int32, (t, t), 1)
        sweep(jnp.where(colf <= row, 0.0, NEG))
        lam = (jnp.exp(jnp.sum(lq1_ref[...] * lk1_ref[...], axis=-1, keepdims=True))
               - jnp.exp(jnp.sum(lq2_ref[...] * lk2_ref[...], axis=-1, keepdims=True))
               + lambda_init)
        outs = []
        for hh in range(DIFF_HPS):
            o = (a_sc[2 * hh] / _row_total(l_sc[2 * hh])
                 - lam * (a_sc[2 * hh + 1] / _row_total(l_sc[2 * hh + 1])))
            ms = jnp.mean(o * o, axis=-1, keepdims=True)
            outs.append(o * lax.rsqrt(ms + LN_EPS) * g_ref[...] * (1.0 - lambda_init))
        o_ref[0] = jnp.concatenate(outs, axis=1).astype(o_ref.dtype)


def _diff_attention(proj3, lq1, lk1, lq2, lk2, g, lambda_init, *, t=1024):
    b, s, _ = proj3.shape
    t = min(t, s)
    qi, ki = _tri_steps(s // t)
    w = 2 * DIFF_HD
    pw = DIFF_HPS * w
    n_hp = DIFF_HEADS // DIFF_HPS
    slopes = jnp.asarray(_alibi_slopes_log2(DIFF_HEADS))
    vec = lambda a: a.reshape(1, -1).astype(F32)
    small = lambda c: pl.BlockSpec((1, c), lambda b_, h, st, qi, ki, sl: (0, 0))
    grid_spec = pltpu.PrefetchScalarGridSpec(
        num_scalar_prefetch=3,
        grid=(b, n_hp, len(qi)),
        in_specs=[pl.BlockSpec((1, t, pw), lambda b_, h, st, qi, ki, sl: (b_, qi[st], h)),
                  pl.BlockSpec((1, t, pw), lambda b_, h, st, qi, ki, sl: (b_, ki[st], n_hp + h)),
                  pl.BlockSpec((1, t, pw), lambda b_, h, st, qi, ki, sl: (b_, ki[st], 2 * n_hp + h)),
                  small(DIFF_HD), small(DIFF_HD), small(DIFF_HD), small(DIFF_HD), small(w)],
        out_specs=pl.BlockSpec((1, t, pw), lambda b_, h, st, qi, ki, sl: (b_, qi[st], h)),
        scratch_shapes=[pltpu.VMEM((2 * DIFF_HPS, t, w), BF16),
                        pltpu.VMEM((2 * DIFF_HPS, t, LANES), F32), pltpu.VMEM((2 * DIFF_HPS, t, LANES), F32),
                        pltpu.VMEM((2 * DIFF_HPS, t, w), F32)],
    )
    return pl.pallas_call(
        functools.partial(_diff_kernel, t=t, lambda_init=lambda_init),
        grid_spec=grid_spec,
        out_shape=jax.ShapeDtypeStruct((b, s, DIFF_V), BF16),
        compiler_params=_cparams(("parallel", "parallel", "arbitrary")),
        name="diff_attention",
    )(jnp.asarray(qi), jnp.asarray(ki), slopes, proj3, proj3, proj3,
      vec(lq1), vec(lk1), vec(lq2), vec(lk2), vec(g))


def _compress_kernel(a_ref, w1_ref, w2_ref, pos_ref, o_ref):
    a = a_ref[0, 0]
    half = CMP_STRIDE * NSA_HD
    top = jnp.dot(a, w1_ref[:half, :], preferred_element_type=F32)
    bot = jnp.dot(a, w1_ref[half:, :], preferred_element_type=F32)
    pos = pos_ref[...]
    pos_hi = pos.astype(BF16)
    pos_lo = (pos - pos_hi.astype(F32)).astype(BF16)
    c = (jnp.dot(pos_hi, w1_ref[...], preferred_element_type=F32)
         + jnp.dot(pos_lo, w1_ref[...], preferred_element_type=F32))[0:1]
    n = a.shape[0]
    hid = top + pltpu.roll(bot, n - 1, axis=0) + c
    hid = hid * jax.nn.sigmoid(hid)
    o_ref[0, 0] = jnp.dot(hid.astype(BF16), w2_ref[...], preferred_element_type=F32).astype(o_ref.dtype)


def _compress(t_halves, pos, w1, w2):
    b, g, nh, _ = t_halves.shape
    pos8 = jnp.broadcast_to(pos.reshape(1, -1).astype(F32), (8, CMP_LEN * NSA_HD))
    return pl.pallas_call(
        _compress_kernel,
        grid=(b, g),
        in_specs=[pl.BlockSpec((1, 1, nh, CMP_STRIDE * NSA_HD), lambda i, j: (i, j, 0, 0)),
                  pl.BlockSpec((CMP_LEN * NSA_HD, PHI_HID), lambda i, j: (0, 0)),
                  pl.BlockSpec((PHI_HID, NSA_HD), lambda i, j: (0, 0)),
                  pl.BlockSpec((8, CMP_LEN * NSA_HD), lambda i, j: (0, 0))],
        out_specs=pl.BlockSpec((1, 1, nh, NSA_HD), lambda i, j: (i, j, 0, 0)),
        out_shape=jax.ShapeDtypeStruct((b, g, nh, NSA_HD), BF16),
        compiler_params=_cparams(("parallel", "parallel")),
        name="nsa_compress",
    )(t_halves, w1.astype(BF16), w2.astype(BF16), pos8)


SEL_LANES = 64


def _cmp_select_kernel(sl_ref, q_ref, kc_ref, vc_ref, o_ref, qs_ref, *, tq, n_cmp_pad, n_slc, n_top):
    g = pl.program_id(1)
    qt = pl.program_id(2)
    t_col = qt * tq + lax.broadcasted_iota(jnp.int32, (tq, 1), 0)
    cmp_pos = lax.broadcasted_iota(jnp.int32, (1, n_cmp_pad), 1) * CMP_STRIDE + (CMP_LEN - 1)
    mask_add = jnp.where(t_col >= cmp_pos, 0.0, NEG)
    row_ok = jnp.where(t_col >= CMP_LEN - 1, 1.0, 0.0)
    cmp_posf = (cmp_pos - qt * tq).astype(F32)
    kc = kc_ref[0, 0]
    vc = vc_ref[0, 0]
    psum = jnp.zeros((tq, n_cmp_pad), F32)
    for hh in range(HPG):
        slope = sl_ref[g * HPG + hh]
        q_h = q_ref[0][:, hh * NSA_HD:(hh + 1) * NSA_HD]
        s = lax.dot_general(q_h, kc, (((1,), (1,)), ((), ())), preferred_element_type=F32)
        s = s + (mask_add + slope * cmp_posf)
        m = jnp.max(s, axis=-1, keepdims=True)
        p = jnp.exp2(s - m)
        l = jnp.sum(p, axis=-1, keepdims=True)
        p = p * (row_ok / l)
        o_ref[0, hh] = jnp.dot(p.astype(BF16), vc, preferred_element_type=F32)
        psum = psum + p
    c_lo = lax.broadcasted_iota(jnp.int32, (n_cmp_pad, 128), 0) * CMP_STRIDE
    s_lo = lax.broadcasted_iota(jnp.int32, (n_cmp_pad, 128), 1) * SLC_LEN
    ov = jnp.minimum(c_lo + CMP_LEN, s_lo + SLC_LEN) - jnp.maximum(c_lo, s_lo)
    cmap = (jnp.maximum(ov, 0).astype(F32) * (1.0 / CMP_LEN)).astype(BF16)
    ps_hi = psum.astype(BF16)
    ps_lo = (psum - ps_hi.astype(F32)).astype(BF16)
    imp = (jnp.dot(ps_hi, cmap, preferred_element_type=F32)
           + jnp.dot(ps_lo, cmap, preferred_element_type=F32))
    imp_t = jnp.transpose(imp)[:n_slc]
    blk = lax.broadcasted_iota(jnp.int32, (n_slc, tq), 0)
    t_row = qt * tq + lax.broadcasted_iota(jnp.int32, (n_slc, tq), 1)
    cur = lax.shift_right_logical(t_row, SLC_SHIFT)
    forced = (blk == 0) | (blk == cur) | (blk == cur - 1)
    x = jnp.where(forced, BIG, jnp.where(blk <= cur, imp_t, NEG))
    sel = jnp.zeros((n_slc, tq), F32)
    for _ in range(n_top):
        mx = jnp.max(x, axis=0, keepdims=True)
        idx = jnp.min(jnp.where(x == mx, blk, n_slc), axis=0, keepdims=True)
        hit = blk == idx
        sel = jnp.where(hit, 1.0, sel)
        x = jnp.where(hit, LOWEST, x)
    if n_slc < 128:
        sel = jnp.concatenate([sel, jnp.zeros((128 - n_slc, tq), F32)], axis=0)
    sel_q = jnp.transpose(sel)[:, :SEL_LANES]
    sel_bias = jnp.where(sel_q > 0.5, 0.0, NEG)
    for hh in range(HPG):
        q_h = q_ref[0][:, hh * NSA_HD:(hh + 1) * NSA_HD]
        qs_ref[0, hh] = jnp.concatenate([q_h.astype(F32), sel_bias], axis=1).astype(qs_ref.dtype)


def _cmp_select(qkv3, k_cmp, v_cmp, *, tq=1024):
    b, s, _ = qkv3.shape
    tq = min(tq, s)
    n_cmp_pad = k_cmp.shape[2]
    gw = HPG * NSA_HD
    q_blk0 = (C_QKV - NSA_Q) // gw
    n_slc = s // SLC_LEN
    n_top = min(SLC_TOPN, n_slc)
    assert n_slc <= SEL_LANES
    slopes = jnp.asarray(_alibi_slopes_log2(NSA_HEADS))
    grid_spec = pltpu.PrefetchScalarGridSpec(
        num_scalar_prefetch=1,
        grid=(b, NSA_KV, s // tq),
        in_specs=[pl.BlockSpec((1, tq, gw), lambda i, j, k, sl: (i, k, q_blk0 + j)),
                  pl.BlockSpec((1, 1, n_cmp_pad, NSA_HD), lambda i, j, k, sl: (j, i, 0, 0)),
                  pl.BlockSpec((1, 1, n_cmp_pad, NSA_HD), lambda i, j, k, sl: (j, i, 0, 0))],
        out_specs=[pl.BlockSpec((1, HPG, tq, NSA_HD), lambda i, j, k, sl: (i, j, k, 0)),
                   pl.BlockSpec((1, HPG, tq, NSA_HD + SEL_LANES), lambda i, j, k, sl: (i, j, k, 0))],
    )
    return pl.pallas_call(
        functools.partial(_cmp_select_kernel, tq=tq, n_cmp_pad=n_cmp_pad, n_slc=n_slc, n_top=n_top),
        grid_spec=grid_spec,
        out_shape=[jax.ShapeDtypeStruct((b, NSA_HEADS, s, NSA_HD), F32),
                   jax.ShapeDtypeStruct((b, NSA_HEADS, s, NSA_HD + SEL_LANES), BF16)],
        compiler_params=_cparams(("parallel", "parallel", "parallel")),
        name="nsa_cmp_select",
    )(slopes, qkv3, k_cmp, v_cmp)


def _slc_kernel(qi_ref, ki_ref, sl_ref, q_ref, k_ref, v_ref, o_ref, m_sc, a_sc, *, t):
    g = pl.program_id(1)
    step = pl.program_id(2)
    qt = qi_ref[step]
    kt = ki_ref[step]

    @pl.when(kt == 0)
    def _():
        m_sc[...] = jnp.full(m_sc.shape, NEG, F32)
        a_sc[...] = jnp.zeros(a_sc.shape, F32)

    col = lax.broadcasted_iota(jnp.int32, (1, t), 1)
    rel = ((kt - qt) * t + col).astype(F32)
    blk_of_key = lax.shift_right_logical(kt * t + lax.broadcasted_iota(jnp.int32, (t, SEL_LANES), 0), SLC_SHIFT)
    onehot = jnp.where(blk_of_key == lax.broadcasted_iota(jnp.int32, (t, SEL_LANES), 1), 1.0, 0.0)
    k = jnp.concatenate([k_ref[0, 0].astype(F32), onehot], axis=1).astype(BF16)
    ones_col = jnp.where(lax.broadcasted_iota(jnp.int32, (t, NSA_HD), 1) == 0, 1.0, 0.0)
    v = jnp.concatenate([v_ref[0, 0].astype(F32), ones_col], axis=1).astype(BF16)

    def sweep(mask_add):
        for hh in range(HPG):
            s = lax.dot_general(q_ref[0, hh], k, (((1,), (1,)), ((), ())), preferred_element_type=F32)
            bias = sl_ref[g * HPG + hh] * rel
            s = s + (bias if mask_add is None else mask_add + bias)
            m_new, _, a_new = _online_update(s, v, m_sc[hh], None, a_sc[hh])
            m_sc[hh] = m_new
            a_sc[hh] = a_new

    @pl.when(kt < qt)
    def _():
        sweep(None)

    @pl.when(kt == qt)
    def _():
        row = lax.broadcasted_iota(jnp.int32, (t, t), 0)
        colf = lax.broadcasted_iota(jnp.int32, (t, t), 1)
        sweep(jnp.where(colf <= row, 0.0, NEG))
        for hh in range(HPG):
            acc = a_sc[hh]
            o_ref[0, hh] = acc[:, :NSA_HD] / acc[:, NSA_HD:NSA_HD + 1]


def _slc_attention(qs, kv4, *, t=1024):
    b, _, s, w = qs.shape
    t = min(t, s)
    qi, ki = _tri_steps(s // t)
    slopes = jnp.asarray(_alibi_slopes_log2(NSA_HEADS))
    grid_spec = pltpu.PrefetchScalarGridSpec(
        num_scalar_prefetch=3,
        grid=(b, NSA_KV, len(qi)),
        in_specs=[pl.BlockSpec((1, HPG, t, w), lambda i, g, st, qi, ki, sl: (i, g, qi[st], 0)),
                  pl.BlockSpec((1, 1, t, NSA_HD), lambda i, g, st, qi, ki, sl: (2 * NSA_KV + g, i, ki[st], 0)),
                  pl.BlockSpec((1, 1, t, NSA_HD), lambda i, g, st, qi, ki, sl: (3 * NSA_KV + g, i, ki[st], 0))],
        out_specs=pl.BlockSpec((1, HPG, t, NSA_HD), lambda i, g, st, qi, ki, sl: (i, g, qi[st], 0)),
        scratch_shapes=[pltpu.VMEM((HPG, t, LANES), F32), pltpu.VMEM((HPG, t, 2 * NSA_HD), F32)],
    )
    return pl.pallas_call(
        functools.partial(_slc_kernel, t=t),
        grid_spec=grid_spec,
        out_shape=jax.ShapeDtypeStruct((b, NSA_HEADS, s, NSA_HD), F32),
        compiler_params=_cparams(("parallel", "parallel", "arbitrary")),
        name="nsa_slc_attention",
    )(jnp.asarray(qi), jnp.asarray(ki), slopes, qs, kv4, kv4)


def _win_kernel(sl_ref, q_ref, *rest, t):
    nblk = WIN // t
    k_refs = rest[:nblk + 1]
    v_refs = rest[nblk + 1:2 * nblk + 2]
    oc_ref, os_ref, g_ref, o_ref = rest[2 * nblk + 2:]
    g = pl.program_id(1)
    qt = pl.program_id(2)
    row = lax.broadcasted_iota(jnp.int32, (t, t), 0)
    col = lax.broadcasted_iota(jnp.int32, (t, t), 1)
    colr = lax.broadcasted_iota(jnp.int32, (1, t), 1)
    refs = tuple(zip(k_refs, v_refs))
    masks = []
    for j in range(len(refs)):
        off = j - nblk
        if off == -nblk:
            masks.append(jnp.where((col > row) & (qt + off >= 0), 0.0, NEG))
        elif off < 0:
            masks.append(jnp.where(qt + off >= 0, 0.0, NEG))
        else:
            masks.append(jnp.where(col <= row, 0.0, NEG))
    outs = []
    for hh in range(HPG):
        slope = sl_ref[g * HPG + hh]
        q = q_ref[0, hh][:, :NSA_HD]
        m = jnp.full((t, LANES), NEG, F32)
        l = jnp.zeros((t, LANES), F32)
        acc = jnp.zeros((t, NSA_HD), F32)
        for j, (k_ref, v_ref) in enumerate(refs):
            off = j - nblk
            s = lax.dot_general(q, k_ref[0, 0], (((1,), (1,)), ((), ())), preferred_element_type=F32)
            s = s + (masks[j] + slope * (off * t + colr).astype(F32))
            m, l, acc = _online_update(s, v_ref[0, 0], m, l, acc)
        o_win = acc / _row_total(l)
        gt = g_ref[0, hh]
        outs.append(gt[:, 0:1] * oc_ref[0, hh] + gt[:, 1:2] * os_ref[0, hh] + gt[:, 2:3] * o_win)
    o_ref[0] = jnp.concatenate(outs, axis=1).astype(o_ref.dtype)


def _win_combine(qs, kv4, o_cmp, o_slc, gates, *, t=512):
    b, _, s, w = qs.shape
    t = min(t, s)
    assert WIN % t == 0
    nblk = WIN // t
    slopes = jnp.asarray(_alibi_slopes_log2(NSA_HEADS))
    kv = lambda kind, off: pl.BlockSpec(
        (1, 1, t, NSA_HD), lambda i, g, qt, sl: (kind * NSA_KV + g, i, jnp.maximum(qt + off, 0), 0))
    per_head = lambda c: pl.BlockSpec((1, HPG, t, c), lambda i, g, qt, sl: (i, g, qt, 0))
    grid_spec = pltpu.PrefetchScalarGridSpec(
        num_scalar_prefetch=1,
        grid=(b, NSA_KV, s // t),
        in_specs=([per_head(w)] + [kv(4, off) for off in range(-nblk, 1)]
                  + [kv(5, off) for off in range(-nblk, 1)]
                  + [per_head(NSA_HD), per_head(NSA_HD), per_head(3)]),
        out_specs=pl.BlockSpec((1, t, HPG * NSA_HD), lambda i, g, qt, sl: (i, qt, g)),
    )
    return pl.pallas_call(
        functools.partial(_win_kernel, t=t),
        grid_spec=grid_spec,
        out_shape=jax.ShapeDtypeStruct((b, s, NSA_Q), BF16),
        compiler_params=_cparams(("parallel", "parallel", "parallel")),
        name="nsa_win_combine",
    )(slopes, qs, *([kv4] * (2 * nblk + 2)), o_cmp, o_slc, gates)


def _mix_kernel(x_ref, od_ref, on_ref, wg1_ref, wg2_ref, bg1_ref, bg2_ref, wd_ref, wn_ref, o_ref, xb_ref):
    @pl.when(pl.program_id(1) == 0)
    def _():
        xb_ref[...] = x_ref[...].astype(BF16)

    xb = xb_ref[...]
    g1 = jax.nn.sigmoid(jnp.dot(xb, wg1_ref[...], preferred_element_type=F32) + bg1_ref[...])
    g2 = jax.nn.sigmoid(jnp.dot(xb, wg2_ref[...], preferred_element_type=F32) + bg2_ref[...])
    bd = jnp.dot(od_ref[...], wd_ref[...], preferred_element_type=F32)
    bn = jnp.dot(on_ref[...], wn_ref[...], preferred_element_type=F32)
    o_ref[...] = (g1 * bd + g2 * bn).astype(o_ref.dtype)


def _mix(x2, o_diff, o_nsa, w_mgate, b_mgate, w_br_diff, w_br_nsa, *, tm=512, tn=512):
    m, d = x2.shape
    tm = min(tm, m)
    nj = d // tn
    wg = w_mgate.astype(BF16)
    bg = b_mgate.reshape(1, -1).astype(F32)
    row = pl.BlockSpec((tm, d), lambda i, j: (i, 0))
    return pl.pallas_call(
        _mix_kernel,
        grid=(m // tm, nj),
        in_specs=[row, row, row,
                  pl.BlockSpec((d, tn), lambda i, j: (0, j)),
                  pl.BlockSpec((d, tn), lambda i, j: (0, j + nj)),
                  pl.BlockSpec((1, tn), lambda i, j: (0, j)),
                  pl.BlockSpec((1, tn), lambda i, j: (0, j + nj)),
                  pl.BlockSpec((d, tn), lambda i, j: (0, j)),
                  pl.BlockSpec((d, tn), lambda i, j: (0, j))],
        out_specs=pl.BlockSpec((tm, tn), lambda i, j: (i, j)),
        out_shape=jax.ShapeDtypeStruct((m, d), BF16),
        scratch_shapes=[pltpu.VMEM((tm, d), BF16)],
        compiler_params=_cparams(("parallel", "arbitrary")),
        name="branch_mix",
    )(x2, o_diff, o_nsa, wg, wg, bg, bg, w_br_diff.astype(BF16), w_br_nsa.astype(BF16))


def _layer_norm_rows(y, g, b):
    mu = jnp.mean(y, axis=-1, keepdims=True)
    yc = y - mu
    var = jnp.mean(yc * yc, axis=-1, keepdims=True)
    return yc * lax.rsqrt(var + LN_EPS) * g + b


def _split_bf16(a):
    hi = a.astype(BF16)
    lo = (a - hi.astype(F32)).astype(BF16)
    return hi, lo


def _post_attn_kernel(mx_ref, x_ref, wo_ref, g_ref, b_ref, wr_hi_ref, wr_lo_ref, br_ref,
                      h_ref, e_ref, gate_ref, rank_ref, cnt_ref, run_ref, *, tm, alpha):
    i = pl.program_id(0)

    @pl.when(i == 0)
    def _():
        run_ref[...] = jnp.zeros(run_ref.shape, F32)

    y = alpha * x_ref[...] + jnp.dot(mx_ref[...], wo_ref[...], preferred_element_type=F32)
    h = _layer_norm_rows(y, g_ref[...], b_ref[...])
    h_ref[...] = h
    h_hi, h_lo = _split_bf16(h)
    logits = (jnp.dot(h_hi, wr_hi_ref[...], preferred_element_type=F32)
              + jnp.dot(h_hi, wr_lo_ref[...], preferred_element_type=F32)
              + jnp.dot(h_lo, wr_hi_ref[...], preferred_element_type=F32)) + br_ref[...]
    lane = lax.broadcasted_iota(jnp.int32, (tm, N_EXPERTS), 1)
    lane4 = lax.broadcasted_iota(jnp.int32, (tm, TOP_K), 1)
    xl = logits
    vals, idxs, hits = [], [], []
    for _ in range(TOP_K):
        mx = jnp.max(xl, axis=-1, keepdims=True)
        idx = jnp.min(jnp.where(xl == mx, lane, N_EXPERTS), axis=-1, keepdims=True)
        hit = lane == idx
        vals.append(mx)
        idxs.append(idx)
        hits.append(hit)
        xl = jnp.where(hit, LOWEST, xl)
    ex = [jnp.exp(v - vals[0]) for v in vals]
    den = ex[0] + ex[1] + ex[2] + ex[3]
    member = jnp.zeros((tm, N_EXPERTS), F32)
    for hit in hits:
        member = jnp.where(hit, 1.0, member)
    r_i = lax.broadcasted_iota(jnp.int32, (tm, tm), 0)
    c_i = lax.broadcasted_iota(jnp.int32, (tm, tm), 1)
    tri = jnp.where(c_i < r_i, 1.0, 0.0).astype(BF16)
    before = jnp.dot(tri, member.astype(BF16), preferred_element_type=F32) + run_ref[...]
    e_out = jnp.zeros((tm, TOP_K), jnp.int32)
    g_out = jnp.zeros((tm, TOP_K), F32)
    r_out = jnp.zeros((tm, TOP_K), jnp.int32)
    for k in range(TOP_K):
        rk = jnp.sum(jnp.where(hits[k], before, 0.0), axis=-1, keepdims=True)
        e_out = jnp.where(lane4 == k, idxs[k], e_out)
        g_out = jnp.where(lane4 == k, ex[k] / den, g_out)
        r_out = jnp.where(lane4 == k, rk.astype(jnp.int32), r_out)
    e_ref[...] = e_out
    gate_ref[...] = g_out
    rank_ref[...] = r_out
    run_ref[...] = run_ref[...] + jnp.sum(member, axis=0, keepdims=True)
    cnt_ref[...] = run_ref[...].astype(jnp.int32)


def _post_attn(mixed, x2, w_o, ln_g, ln_b, w_router, b_router, *, alpha, tm=256):
    m, d = x2.shape
    tm = min(tm, m)
    wr_hi, wr_lo = _split_bf16(w_router.astype(F32))
    vec = lambda a: a.reshape(1, -1).astype(F32)
    full = lambda r, c: pl.BlockSpec((r, c), lambda i: (0, 0))
    rows = lambda c: pl.BlockSpec((tm, c), lambda i: (i, 0))
    return pl.pallas_call(
        functools.partial(_post_attn_kernel, tm=tm, alpha=alpha),
        grid=(m // tm,),
        in_specs=[rows(d), rows(d), full(d, d), full(1, d), full(1, d),
                  full(d, N_EXPERTS), full(d, N_EXPERTS), full(1, N_EXPERTS)],
        out_specs=[rows(d), rows(TOP_K), rows(TOP_K), rows(TOP_K), full(1, N_EXPERTS)],
        out_shape=[jax.ShapeDtypeStruct((m, d), F32),
                   jax.ShapeDtypeStruct((m, TOP_K), jnp.int32),
                   jax.ShapeDtypeStruct((m, TOP_K), F32),
                   jax.ShapeDtypeStruct((m, TOP_K), jnp.int32),
                   jax.ShapeDtypeStruct((1, N_EXPERTS), jnp.int32)],
        scratch_shapes=[pltpu.VMEM((1, N_EXPERTS), F32)],
        compiler_params=_cparams(("arbitrary",)),
        name="post_attn_router",
    )(mixed, x2, w_o.astype(BF16), vec(ln_g), vec(ln_b), wr_hi, wr_lo, vec(b_router))


def _dest_kernel(ps_ref, e_ref, r_ref, o_ref):
    e = e_ref[...]
    base = jnp.zeros(e.shape, jnp.int32)
    for j in range(N_EXPERTS):
        base = jnp.where(e == j, ps_ref[j], base)
    o_ref[...] = base + r_ref[...]


def _dest(top_e, rank, pstart, *, tm=2048):
    m = top_e.shape[0]
    tm = min(tm, m)
    grid_spec = pltpu.PrefetchScalarGridSpec(
        num_scalar_prefetch=1,
        grid=(m // tm,),
        in_specs=[pl.BlockSpec((tm, TOP_K), lambda i, ps: (i, 0)),
                  pl.BlockSpec((tm, TOP_K), lambda i, ps: (i, 0))],
        out_specs=pl.BlockSpec((tm, TOP_K), lambda i, ps: (i, 0)),
    )
    return pl.pallas_call(
        _dest_kernel, grid_spec=grid_spec,
        out_shape=jax.ShapeDtypeStruct((m, TOP_K), jnp.int32),
        compiler_params=_cparams(("parallel",)),
        name="moe_dest",
    )(pstart, top_e, rank)


N_DMA_PRIORITIES = 2


def _dispatch_kernel(zf_ref, dest_ref, h_ref, xs_ref, zeros, sem, zsem, *, tm, n_blocks):
    @pl.when(pl.program_id(0) == 0)
    def _():
        zeros[...] = jnp.zeros(zeros.shape, zeros.dtype)

        def zero_copy(blk):
            start = pl.multiple_of(blk * MOE_BLOCK, MOE_BLOCK)
            return pltpu.make_async_copy(zeros, xs_ref.at[pl.ds(start, MOE_BLOCK), :], zsem)

        def zstart(blk, carry):
            @pl.when(zf_ref[blk] > 0)
            def _():
                zero_copy(blk).start()
            return carry

        def zwait(blk, carry):
            @pl.when(zf_ref[blk] > 0)
            def _():
                zero_copy(blk).wait()
            return carry

        lax.fori_loop(0, n_blocks, zstart, 0)
        lax.fori_loop(0, n_blocks, zwait, 0)

    def row_copy(r, k):
        return pltpu.make_async_copy(h_ref.at[pl.ds(r, 1), :],
                                     xs_ref.at[pl.ds(dest_ref[r * TOP_K + k], 1), :], sem)

    for r in range(tm):
        for k in range(TOP_K):
            row_copy(r, k).start(priority=k % N_DMA_PRIORITIES)
    for r in range(tm):
        for k in range(TOP_K):
            row_copy(r, k).wait()


def _dispatch(h, dest_flat, zero_flag, n_pad, *, tm=256):
    m, d = h.shape
    tm = min(tm, m)
    assert m % tm == 0
    n_blocks = n_pad // MOE_BLOCK
    grid_spec = pltpu.PrefetchScalarGridSpec(
        num_scalar_prefetch=1,
        grid=(m // tm,),
        in_specs=[pl.BlockSpec((tm * TOP_K,), lambda i, zf: (i,), memory_space=pltpu.SMEM),
                  pl.BlockSpec((tm, d), lambda i, zf: (i, 0))],
        out_specs=pl.BlockSpec(memory_space=pl.ANY),
        scratch_shapes=[pltpu.VMEM((MOE_BLOCK, d), h.dtype), pltpu.SemaphoreType.DMA(()),
                        pltpu.SemaphoreType.DMA(())],
    )
    return pl.pallas_call(
        functools.partial(_dispatch_kernel, tm=tm, n_blocks=n_blocks),
        grid_spec=grid_spec,
        out_shape=jax.ShapeDtypeStruct((n_pad, d), h.dtype),
        compiler_params=_cparams(("arbitrary",)),
        name="moe_dispatch",
    )(zero_flag, dest_flat, h)


def _expert_kernel(be_ref, nu_ref, xs_ref, w1_ref, b1_ref, w2_ref, b2_ref, y_ref, w1b, w2b):
    i = pl.program_id(0)
    new_expert = (i == 0) | (be_ref[i] != be_ref[jnp.maximum(i - 1, 0)])

    @pl.when((i < nu_ref[0]) & new_expert)
    def _():
        w1b[...] = w1_ref[0].astype(BF16)
        w2b[...] = w2_ref[0].astype(BF16)

    @pl.when(i < nu_ref[0])
    def _():
        xb = xs_ref[...].astype(BF16)
        hb = jnp.dot(xb, w1b[...], preferred_element_type=F32) + b1_ref[0]
        glu = jnp.minimum(hb[:, :D_FF], SWIGLU_LIMIT)
        lin = jnp.clip(hb[:, D_FF:], -SWIGLU_LIMIT, SWIGLU_LIMIT)
        act = glu * jax.nn.sigmoid(SWIGLU_ALPHA * glu) * (lin + 1.0)
        y_ref[...] = jnp.dot(act.astype(BF16), w2b[...], preferred_element_type=F32) + b2_ref[0]

    @pl.when(i >= nu_ref[0])
    def _():
        y_ref[...] = jnp.zeros(y_ref.shape, y_ref.dtype)


def _experts(xs, blk_expert, n_used, w1, b1, w2, b2):
    n_pad, d = xs.shape
    n_blocks = n_pad // MOE_BLOCK
    last = lambda i, nu: jnp.minimum(i, nu[0] - 1)
    grid_spec = pltpu.PrefetchScalarGridSpec(
        num_scalar_prefetch=2,
        grid=(n_blocks,),
        in_specs=[pl.BlockSpec((MOE_BLOCK, d), lambda i, be, nu: (last(i, nu), 0)),
                  pl.BlockSpec((1, d, 2 * D_FF), lambda i, be, nu: (be[last(i, nu)], 0, 0)),
                  pl.BlockSpec((1, 1, 2 * D_FF), lambda i, be, nu: (be[last(i, nu)], 0, 0)),
                  pl.BlockSpec((1, D_FF, d), lambda i, be, nu: (be[last(i, nu)], 0, 0)),
                  pl.BlockSpec((1, 1, d), lambda i, be, nu: (be[last(i, nu)], 0, 0))],
        out_specs=pl.BlockSpec((MOE_BLOCK, d), lambda i, be, nu: (i, 0)),
        scratch_shapes=[pltpu.VMEM((d, 2 * D_FF), BF16), pltpu.VMEM((D_FF, d), BF16)],
    )
    return pl.pallas_call(
        _expert_kernel, grid_spec=grid_spec,
        out_shape=jax.ShapeDtypeStruct((n_pad, d), F32),
        compiler_params=_cparams(("arbitrary",), vmem=EXPERT_VMEM_LIMIT),
        name="moe_experts",
    )(blk_expert, n_used, xs, w1.astype(F32), b1.reshape(N_EXPERTS, 1, -1).astype(F32),
      w2.astype(F32), b2.reshape(N_EXPERTS, 1, -1).astype(F32))


def _final_kernel(dest_ref, h_ref, gate_ref, p_ref, wpg_ref, bpg_ref, wpp_ref, g_ref, b_ref, y_ref,
                  o_ref, buf, sem, *, tm, alpha):
    def row_copy(r, k):
        return pltpu.make_async_copy(y_ref.at[pl.ds(dest_ref[r * TOP_K + k], 1), :],
                                     buf.at[k, pl.ds(r, 1), :], sem)

    for r in range(tm):
        for k in range(TOP_K):
            row_copy(r, k).start(priority=k % N_DMA_PRIORITIES)
    h = h_ref[...]
    ple = (jax.nn.sigmoid(jnp.dot(h.astype(BF16), wpg_ref[...], preferred_element_type=F32) + bpg_ref[...])
           * jnp.dot(p_ref[...].astype(BF16), wpp_ref[...], preferred_element_type=F32))
    for r in range(tm):
        for k in range(TOP_K):
            row_copy(r, k).wait()
    gate = gate_ref[...]
    ffn = gate[:, 0:1] * buf[0]
    for k in range(1, TOP_K):
        ffn = ffn + gate[:, k:k + 1] * buf[k]
    o_ref[...] = _layer_norm_rows(alpha * h + ffn + ple, g_ref[...], b_ref[...])


def _final(h, gate, dest_flat, y, p2, w_ple_gate, b_ple_gate, w_ple_proj, ln_g, ln_b, *, alpha, tm=256):
    m, d = h.shape
    tm = min(tm, m)
    vec = lambda a: a.reshape(1, -1).astype(F32)
    full = lambda r, c: pl.BlockSpec((r, c), lambda i: (0, 0))
    rows = lambda c: pl.BlockSpec((tm, c), lambda i: (i, 0))
    return pl.pallas_call(
        functools.partial(_final_kernel, tm=tm, alpha=alpha),
        grid=(m // tm,),
        in_specs=[pl.BlockSpec((tm * TOP_K,), lambda i: (i,), memory_space=pltpu.SMEM),
                  rows(d), rows(TOP_K), rows(PLE_DIM), full(d, d), full(1, d), full(PLE_DIM, d),
                  full(1, d), full(1, d), pl.BlockSpec(memory_space=pl.ANY)],
        out_specs=rows(d),
        out_shape=jax.ShapeDtypeStruct((m, d), F32),
        scratch_shapes=[pltpu.VMEM((TOP_K, tm, d), F32), pltpu.SemaphoreType.DMA(())],
        compiler_params=_cparams(("arbitrary",)),
        name="moe_combine_final",
    )(dest_flat, h, gate, p2, w_ple_gate.astype(BF16), vec(b_ple_gate), w_ple_proj.astype(BF16),
      vec(ln_g), vec(ln_b), y)


def _attention_block(x2, b, s, w_in, diff_p, nsa_p, lambda_init):
    t = b * s
    qkv, kv, gates = _input_projection(x2, w_in)
    qkv3 = qkv.reshape(b, s, C_QKV)
    kv4 = kv.reshape(N_KV, b, s, NSA_HD)
    o_diff = _diff_attention(qkv3, *diff_p, lambda_init)

    halves = lambda kind: kv[kind * NSA_KV:(kind + 1) * NSA_KV].reshape(
        NSA_KV, b, s // CMP_STRIDE, CMP_STRIDE * NSA_HD)
    pos_k, pos_v, phi_k1, phi_k2, phi_v1, phi_v2 = nsa_p
    k_cmp = _compress(halves(0), pos_k, phi_k1, phi_k2)
    v_cmp = _compress(halves(1), pos_v, phi_v1, phi_v2)
    o_cmp, qs = _cmp_select(qkv3, k_cmp, v_cmp)
    o_slc = _slc_attention(qs, kv4)
    g4 = gates[:, :NSA_GATES].reshape(b, s, NSA_HEADS, 3).transpose(0, 2, 1, 3)
    o_nsa = _win_combine(qs, kv4, o_cmp, o_slc, g4)
    return o_diff.reshape(t, DIFF_V), o_nsa.reshape(t, NSA_Q)


def _moe_block(h, top_e, gate, rank, counts, w1, b1, w2, b2):
    t = h.shape[0]
    n_blocks = (t * TOP_K + MOE_BLOCK - 1) // MOE_BLOCK + N_EXPERTS
    n_pad = n_blocks * MOE_BLOCK
    counts = counts.reshape(N_EXPERTS)
    pcounts = (counts + MOE_BLOCK - 1) // MOE_BLOCK * MOE_BLOCK
    pend = jnp.cumsum(pcounts)
    pstart = (pend - pcounts).astype(jnp.int32)
    blk_start = jnp.arange(n_blocks, dtype=jnp.int32) * MOE_BLOCK
    blk_expert = jnp.minimum(jnp.sum(pend[None, :] <= blk_start[:, None], axis=1),
                             N_EXPERTS - 1).astype(jnp.int32)
    n_used = (pend[-1:] // MOE_BLOCK).astype(jnp.int32)
    zero_flag = ((blk_start + MOE_BLOCK == pend[blk_expert])
                 | (blk_start >= pend[-1])).astype(jnp.int32)
    dest = _dest(top_e, rank, pstart).reshape(t * TOP_K)
    xs = _dispatch(h, dest, zero_flag, n_pad)
    y = _experts(xs, blk_expert, n_used, w1, b1, w2, b2)
    return dest, y


def kernel(x, p, w_in, diff_lq1, diff_lk1, diff_lq2, diff_lk2, diff_subln_g, nsa_pos_k, nsa_pos_v,
           nsa_phi_k1, nsa_phi_k2, nsa_phi_v1, nsa_phi_v2, w_br_diff, w_br_nsa, w_mgate, b_mgate, w_o,
           ln1_g, ln1_b, w_router, b_router, w_e1, b_e1, w_e2, b_e2, w_ple_gate, b_ple_gate,
           w_ple_proj, ln2_g, ln2_b):
    b, s, d = x.shape
    depth = w_in.shape[0]
    alpha = (2.0 * depth) ** 0.25
    x2 = x.reshape(b * s, d)
    for i in range(depth):
        lambda_init = 0.8 - 0.6 * math.exp(-0.3 * i)
        o_diff, o_nsa = _attention_block(
            x2, b, s, w_in[i],
            (diff_lq1[i], diff_lk1[i], diff_lq2[i], diff_lk2[i], diff_subln_g[i]),
            (nsa_pos_k[i], nsa_pos_v[i], nsa_phi_k1[i], nsa_phi_k2[i], nsa_phi_v1[i], nsa_phi_v2[i]),
            lambda_init)
        mixed = _mix(x2, o_diff, o_nsa, w_mgate[i], b_mgate[i], w_br_diff[i], w_br_nsa[i])
        h, top_e, gate, rank, counts = _post_attn(mixed, x2, w_o[i], ln1_g[i], ln1_b[i],
                                                  w_router[i], b_router[i], alpha=alpha)
        dest, y = _moe_block(h, top_e, gate, rank, counts, w_e1[i], b_e1[i], w_e2[i], b_e2[i])
        x2 = _final(h, gate, dest, y, p[i].reshape(b * s, -1), w_ple_gate[i], b_ple_gate[i],
                    w_ple_proj[i], ln2_g[i], ln2_b[i], alpha=alpha)
    return x2.reshape(b, s, d)
```

```python
import functools
import math

import numpy as np
import jax
import jax.numpy as jnp
from jax import lax
from jax.experimental import pallas as pl
from jax.experimental.pallas import tpu as pltpu

D_MODEL = 1024
DIFF_HEADS = 8
DIFF_HD = 64
NSA_HEADS = 16
NSA_KV = 4
NSA_HD = 64
HPG = NSA_HEADS // NSA_KV
CMP_LEN = 32
CMP_STRIDE = 16
SLC_LEN = 64
SLC_SHIFT = 6
SLC_TOPN = 16
WIN = 512
PHI_HID = 256
N_EXPERTS = 32
TOP_K = 4
D_FF = 1024
SWIGLU_LIMIT = 7.0
SWIGLU_ALPHA = 1.702
MOE_BLOCK = 512
PLE_DIM = 256
LN_EPS = 1e-5
NEG = -1e30
BIG = 1e30
LOWEST = -3e38
LOG2E = 1.4426950408889634

DIFF_QK = DIFF_HEADS * 2 * DIFF_HD
DIFF_V = DIFF_HEADS * 2 * DIFF_HD
NSA_Q = NSA_HEADS * NSA_HD
NSA_KVW = NSA_KV * NSA_HD
NSA_GATES = NSA_HEADS * 3
IN_SIZES = (DIFF_QK, DIFF_QK, DIFF_V, NSA_Q) + (NSA_KVW,) * 6 + (NSA_GATES,)
C_MAIN = sum(IN_SIZES[:-1])

VMEM_LIMIT = 48 * 1024 * 1024
EXPERT_VMEM_LIMIT = 56 * 1024 * 1024

BF16 = jnp.bfloat16
F32 = jnp.float32


def _alibi_slopes_log2(n):
    return (2.0 ** (-8.0 * np.arange(1, n + 1) / n) * LOG2E).astype(np.float32)


def _cparams(sem, vmem=VMEM_LIMIT):
    return pltpu.CompilerParams(dimension_semantics=sem, vmem_limit_bytes=vmem)


PROJ_CHUNK = 512
GATE_PAD = 128
C_QKV = 2 * DIFF_QK + DIFF_V + NSA_Q
N_KV = (C_MAIN - C_QKV) // NSA_HD


def _proj_kernel(x_ref, w_ref, s_ref, o_ref, kv_ref, g_ref):
    xb = x_ref[...].astype(BF16)
    for c in range(C_MAIN // PROJ_CHUNK):
        cs = slice(c * PROJ_CHUNK, (c + 1) * PROJ_CHUNK)
        acc = jnp.dot(xb, w_ref[:, cs], preferred_element_type=F32)
        if c < C_QKV // PROJ_CHUNK:
            o_ref[:, cs] = (acc * s_ref[:, cs]).astype(o_ref.dtype)
        else:
            per_chunk = PROJ_CHUNK // NSA_HD
            for j in range(per_chunk):
                stream = (c - C_QKV // PROJ_CHUNK) * per_chunk + j
                kv_ref[stream] = acc[:, j * NSA_HD:(j + 1) * NSA_HD].astype(kv_ref.dtype)
    gl = jnp.dot(xb, w_ref[:, C_MAIN:], preferred_element_type=F32)
    g_ref[...] = jax.nn.sigmoid(gl)


def _input_projection(x2, w_in, *, tm=512):
    m, k = x2.shape
    tm = min(tm, m)
    assert m % tm == 0 and C_MAIN % PROJ_CHUNK == 0 and C_QKV % PROJ_CHUNK == 0
    w = jnp.pad(w_in, ((0, 0), (0, GATE_PAD - NSA_GATES))).astype(BF16)
    scale = np.ones((1, C_QKV), np.float32)
    scale[0, :DIFF_QK] = DIFF_HD ** -0.5 * LOG2E
    scale[0, C_QKV - NSA_Q:] = NSA_HD ** -0.5 * LOG2E
    n_all = C_MAIN + GATE_PAD
    return pl.pallas_call(
        _proj_kernel,
        grid=(m // tm,),
        in_specs=[pl.BlockSpec((tm, k), lambda i: (i, 0)),
                  pl.BlockSpec((k, n_all), lambda i: (0, 0)),
                  pl.BlockSpec((1, C_QKV), lambda i: (0, 0))],
        out_specs=[pl.BlockSpec((tm, C_QKV), lambda i: (i, 0)),
                   pl.BlockSpec((N_KV, tm, NSA_HD), lambda i: (0, i, 0)),
                   pl.BlockSpec((tm, GATE_PAD), lambda i: (i, 0))],
        out_shape=[jax.ShapeDtypeStruct((m, C_QKV), BF16),
                   jax.ShapeDtypeStruct((N_KV, m, NSA_HD), BF16),
                   jax.ShapeDtypeStruct((m, GATE_PAD), F32)],
        compiler_params=_cparams(("parallel",)),
        name="input_projection",
    )(x2, w, jnp.asarray(scale))


LANES = 128


def _online_update(s, v, m_old, l_old, acc_old):
    tk = s.shape[1]
    hd = acc_old.shape[1]
    m_new = jnp.maximum(m_old, jnp.max(s, axis=-1, keepdims=True))
    alpha = jnp.exp2(m_old - m_new)
    p = jnp.exp2(s - jnp.concatenate([m_new] * (tk // LANES), axis=1))
    l_new = None
    if l_old is not None:
        psum = p[:, :LANES]
        for c in range(1, tk // LANES):
            psum = psum + p[:, c * LANES:(c + 1) * LANES]
        l_new = alpha * l_old + psum
    acc_new = alpha[:, :hd] * acc_old + jnp.dot(p.astype(BF16), v, preferred_element_type=F32)
    return m_new, l_new, acc_new


def _row_total(l):
    return jnp.sum(l, axis=-1, keepdims=True)


def _tri_steps(n):
    qi, ki = [], []
    for q in range(n):
        for k in range(q + 1):
            qi.append(q)
            ki.append(k)
    return np.asarray(qi, np.int32), np.asarray(ki, np.int32)


DIFF_HPS = 2


def _diff_kernel(qi_ref, ki_ref, sl_ref, q_ref, k_ref, v_ref,
                 lq1_ref, lk1_ref, lq2_ref, lk2_ref, g_ref, o_ref,
                 qm_sc, m_sc, l_sc, a_sc, *, t, lambda_init):
    hp = pl.program_id(1)
    step = pl.program_id(2)
    qt = qi_ref[step]
    kt = ki_ref[step]
    w = 2 * DIFF_HD

    @pl.when(kt == 0)
    def _():
        for hh in range(DIFF_HPS):
            q = q_ref[0][:, hh * w:(hh + 1) * w]
            lane = lax.broadcasted_iota(jnp.int32, q.shape, 1)
            qm_sc[2 * hh] = jnp.where(lane < DIFF_HD, q, jnp.zeros_like(q))
            qm_sc[2 * hh + 1] = jnp.where(lane >= DIFF_HD, q, jnp.zeros_like(q))
        m_sc[...] = jnp.full(m_sc.shape, NEG, F32)
        l_sc[...] = jnp.zeros(l_sc.shape, F32)
        a_sc[...] = jnp.zeros(a_sc.shape, F32)

    col = lax.broadcasted_iota(jnp.int32, (1, t), 1)
    rel = ((kt - qt) * t + col).astype(F32)

    def sweep(mask_add):
        for hh in range(DIFF_HPS):
            bias = sl_ref[hp * DIFF_HPS + hh] * rel
            bias = bias if mask_add is None else mask_add + bias
            k = k_ref[0][:, hh * w:(hh + 1) * w]
            v = v_ref[0][:, hh * w:(hh + 1) * w]
            for c in range(2):
                i = 2 * hh + c
                s = lax.dot_general(qm_sc[i], k, (((1,), (1,)), ((), ())), preferred_element_type=F32) + bias
                m_new, l_new, a_new = _online_update(s, v, m_sc[i], l_sc[i], a_sc[i])
                m_sc[i] = m_new
                l_sc[i] = l_new
                a_sc[i] = a_new

    @pl.when(kt < qt)
    def _():
        sweep(None)

    @pl.when(kt == qt)
    def _():
        row = lax.broadcasted_iota(jnp.int32, (t, t), 0)
        colf = lax.broadcasted_iota(jnp.int32, (t, t), 1)
        sweep(jnp.where(colf <= row, 0.0, NEG))
        lam = (jnp.exp(jnp.sum(lq1_ref[...] * lk1_ref[...], axis=-1, keepdims=True))
               - jnp.exp(jnp.sum(lq2_ref[...] * lk2_ref[...], axis=-1, keepdims=True))
               + lambda_init)
        outs = []
        for hh in range(DIFF_HPS):
            o = (a_sc[2 * hh] / _row_total(l_sc[2 * hh])
                 - lam * (a_sc[2 * hh + 1] / _row_total(l_sc[2 * hh + 1])))
            ms = jnp.mean(o * o, axis=-1, keepdims=True)
            outs.append(o * lax.rsqrt(ms + LN_EPS) * g_ref[...] * (1.0 - lambda_init))
        o_ref[0] = jnp.concatenate(outs, axis=1).astype(o_ref.dtype)


def _diff_attention(proj3, lq1, lk1, lq2, lk2, g, lambda_init, *, t=1024):
    b, s, _ = proj3.shape
    t = min(t, s)
    qi, ki = _tri_steps(s // t)
    w = 2 * DIFF_HD
    pw = DIFF_HPS * w
    n_hp = DIFF_HEADS // DIFF_HPS
    slopes = jnp.asarray(_alibi_slopes_log2(DIFF_HEADS))
    vec = lambda a: a.reshape(1, -1).astype(F32)
    small = lambda c: pl.BlockSpec((1, c), lambda b_, h, st, qi, ki, sl: (0, 0))
    grid_spec = pltpu.PrefetchScalarGridSpec(
        num_scalar_prefetch=3,
        grid=(b, n_hp, len(qi)),
        in_specs=[pl.BlockSpec((1, t, pw), lambda b_, h, st, qi, ki, sl: (b_, qi[st], h)),
                  pl.BlockSpec((1, t, pw), lambda b_, h, st, qi, ki, sl: (b_, ki[st], n_hp + h)),
                  pl.BlockSpec((1, t, pw), lambda b_, h, st, qi, ki, sl: (b_, ki[st], 2 * n_hp + h)),
                  small(DIFF_HD), small(DIFF_HD), small(DIFF_HD), small(DIFF_HD), small(w)],
        out_specs=pl.BlockSpec((1, t, pw), lambda b_, h, st, qi, ki, sl: (b_, qi[st], h)),
        scratch_shapes=[pltpu.VMEM((2 * DIFF_HPS, t, w), BF16),
                        pltpu.VMEM((2 * DIFF_HPS, t, LANES), F32), pltpu.VMEM((2 * DIFF_HPS, t, LANES), F32),
                        pltpu.VMEM((2 * DIFF_HPS, t, w), F32)],
    )
    return pl.pallas_call(
        functools.partial(_diff_kernel, t=t, lambda_init=lambda_init),
        grid_spec=grid_spec,
        out_shape=jax.ShapeDtypeStruct((b, s, DIFF_V), BF16),
        compiler_params=_cparams(("parallel", "parallel", "arbitrary")),
        name="diff_attention",
    )(jnp.asarray(qi), jnp.asarray(ki), slopes, proj3, proj3, proj3,
      vec(lq1), vec(lk1), vec(lq2), vec(lk2), vec(g))


def _compress_kernel(a_ref, w1_ref, w2_ref, pos_ref, o_ref):
    a = a_ref[0, 0]
    half = CMP_STRIDE * NSA_HD
    top = jnp.dot(a, w1_ref[:half, :], preferred_element_type=F32)
    bot = jnp.dot(a, w1_ref[half:, :], preferred_element_type=F32)
    pos = pos_ref[...]
    pos_hi = pos.astype(BF16)
    pos_lo = (pos - pos_hi.astype(F32)).astype(BF16)
    c = (jnp.dot(pos_hi, w1_ref[...], preferred_element_type=F32)
         + jnp.dot(pos_lo, w1_ref[...], preferred_element_type=F32))[0:1]
    n = a.shape[0]
    hid = top + pltpu.roll(bot, n - 1, axis=0) + c
    hid = hid * jax.nn.sigmoid(hid)
    o_ref[0, 0] = jnp.dot(hid.astype(BF16), w2_ref[...], preferred_element_type=F32).astype(o_ref.dtype)


def _compress(t_halves, pos, w1, w2):
    b, g, nh, _ = t_halves.shape
    pos8 = jnp.broadcast_to(pos.reshape(1, -1).astype(F32), (8, CMP_LEN * NSA_HD))
    return pl.pallas_call(
        _compress_kernel,
        grid=(b, g),
        in_specs=[pl.BlockSpec((1, 1, nh, CMP_STRIDE * NSA_HD), lambda i, j: (i, j, 0, 0)),
                  pl.BlockSpec((CMP_LEN * NSA_HD, PHI_HID), lambda i, j: (0, 0)),
                  pl.BlockSpec((PHI_HID, NSA_HD), lambda i, j: (0, 0)),
                  pl.BlockSpec((8, CMP_LEN * NSA_HD), lambda i, j: (0, 0))],
        out_specs=pl.BlockSpec((1, 1, nh, NSA_HD), lambda i, j: (i, j, 0, 0)),
        out_shape=jax.ShapeDtypeStruct((b, g, nh, NSA_HD), BF16),
        compiler_params=_cparams(("parallel", "parallel")),
        name="nsa_compress",
    )(t_halves, w1.astype(BF16), w2.astype(BF16), pos8)


SEL_LANES = 64


def _cmp_select_kernel(sl_ref, q_ref, kc_ref, vc_ref, o_ref, qs_ref, *, tq, n_cmp_pad, n_slc, n_top):
    g = pl.program_id(1)
    qt = pl.program_id(2)
    t_col = qt * tq + lax.broadcasted_iota(jnp.int32, (tq, 1), 0)
    cmp_pos = lax.broadcasted_iota(jnp.int32, (1, n_cmp_pad), 1) * CMP_STRIDE + (CMP_LEN - 1)
    mask_add = jnp.where(t_col >= cmp_pos, 0.0, NEG)
    row_ok = jnp.where(t_col >= CMP_LEN - 1, 1.0, 0.0)
    cmp_posf = (cmp_pos - qt * tq).astype(F32)
    kc = kc_ref[0, 0]
    vc = vc_ref[0, 0]
    psum = jnp.zeros((tq, n_cmp_pad), F32)
    for hh in range(HPG):
        slope = sl_ref[g * HPG + hh]
        q_h = q_ref[0][:, hh * NSA_HD:(hh + 1) * NSA_HD]
        s = lax.dot_general(q_h, kc, (((1,), (1,)), ((), ())), preferred_element_type=F32)
        s = s + (mask_add + slope * cmp_posf)
        m = jnp.max(s, axis=-1, keepdims=True)
        p = jnp.exp2(s - m)
        l = jnp.sum(p, axis=-1, keepdims=True)
        p = p * (row_ok / l)
        o_ref[0, hh] = jnp.dot(p.astype(BF16), vc, preferred_element_type=F32)
        psum = psum + p
    c_lo = lax.broadcasted_iota(jnp.int32, (n_cmp_pad, 128), 0) * CMP_STRIDE
    s_lo = lax.broadcasted_iota(jnp.int32, (n_cmp_pad, 128), 1) * SLC_LEN
    ov = jnp.minimum(c_lo + CMP_LEN, s_lo + SLC_LEN) - jnp.maximum(c_lo, s_lo)
    cmap = (jnp.maximum(ov, 0).astype(F32) * (1.0 / CMP_LEN)).astype(BF16)
    ps_hi = psum.astype(BF16)
    ps_lo = (psum - ps_hi.astype(F32)).astype(BF16)
    imp = (jnp.dot(ps_hi, cmap, preferred_element_type=F32)
           + jnp.dot(ps_lo, cmap, preferred_element_type=F32))
    imp_t = jnp.transpose(imp)[:n_slc]
    blk = lax.broadcasted_iota(jnp.int32, (n_slc, tq), 0)
    t_row = qt * tq + lax.broadcasted_iota(jnp.int32, (n_slc, tq), 1)
    cur = lax.shift_right_logical(t_row, SLC_SHIFT)
    forced = (blk == 0) | (blk == cur) | (blk == cur - 1)
    x = jnp.where(forced, BIG, jnp.where(blk <= cur, imp_t, NEG))
    sel = jnp.zeros((n_slc, tq), F32)
    for _ in range(n_top):
        mx = jnp.max(x, axis=0, keepdims=True)
        idx = jnp.min(jnp.where(x == mx, blk, n_slc), axis=0, keepdims=True)
        hit = blk == idx
        sel = jnp.where(hit, 1.0, sel)
        x = jnp.where(hit, LOWEST, x)
    if n_slc < 128:
        sel = jnp.concatenate([sel, jnp.zeros((128 - n_slc, tq), F32)], axis=0)
    sel_q = jnp.transpose(sel)[:, :SEL_LANES]
    sel_bias = jnp.where(sel_q > 0.5, 0.0, NEG)
    for hh in range(HPG):
        q_h = q_ref[0][:, hh * NSA_HD:(hh + 1) * NSA_HD]
        qs_ref[0, hh] = jnp.concatenate([q_h.astype(F32), sel_bias], axis=1).astype(qs_ref.dtype)


def _cmp_select(qkv3, k_cmp, v_cmp, *, tq=1024):
    b, s, _ = qkv3.shape
    tq = min(tq, s)
    n_cmp_pad = k_cmp.shape[2]
    gw = HPG * NSA_HD
    q_blk0 = (C_QKV - NSA_Q) // gw
    n_slc = s // SLC_LEN
    n_top = min(SLC_TOPN, n_slc)
    assert n_slc <= SEL_LANES
    slopes = jnp.asarray(_alibi_slopes_log2(NSA_HEADS))
    grid_spec = pltpu.PrefetchScalarGridSpec(
        num_scalar_prefetch=1,
        grid=(b, NSA_KV, s // tq),
        in_specs=[pl.BlockSpec((1, tq, gw), lambda i, j, k, sl: (i, k, q_blk0 + j)),
                  pl.BlockSpec((1, 1, n_cmp_pad, NSA_HD), lambda i, j, k, sl: (j, i, 0, 0)),
                  pl.BlockSpec((1, 1, n_cmp_pad, NSA_HD), lambda i, j, k, sl: (j, i, 0, 0))],
        out_specs=[pl.BlockSpec((1, HPG, tq, NSA_HD), lambda i, j, k, sl: (i, j, k, 0)),
                   pl.BlockSpec((1, HPG, tq, NSA_HD + SEL_LANES), lambda i, j, k, sl: (i, j, k, 0))],
    )
    return pl.pallas_call(
        functools.partial(_cmp_select_kernel, tq=tq, n_cmp_pad=n_cmp_pad, n_slc=n_slc, n_top=n_top),
        grid_spec=grid_spec,
        out_shape=[jax.ShapeDtypeStruct((b, NSA_HEADS, s, NSA_HD), F32),
                   jax.ShapeDtypeStruct((b, NSA_HEADS, s, NSA_HD + SEL_LANES), BF16)],
        compiler_params=_cparams(("parallel", "parallel", "parallel")),
        name="nsa_cmp_select",
    )(slopes, qkv3, k_cmp, v_cmp)


def _slc_kernel(qi_ref, ki_ref, sl_ref, q_ref, k_ref, v_ref, o_ref, m_sc, a_sc, *, t):
    g = pl.program_id(1)
    step = pl.program_id(2)
    qt = qi_ref[step]
    kt = ki_ref[step]

    @pl.when(kt == 0)
    def _():
        m_sc[...] = jnp.full(m_sc.shape, NEG, F32)
        a_sc[...] = jnp.zeros(a_sc.shape, F32)

    col = lax.broadcasted_iota(jnp.int32, (1, t), 1)
    rel = ((kt - qt) * t + col).astype(F32)
    blk_of_key = lax.shift_right_logical(kt * t + lax.broadcasted_iota(jnp.int32, (t, SEL_LANES), 0), SLC_SHIFT)
    onehot = jnp.where(blk_of_key == lax.broadcasted_iota(jnp.int32, (t, SEL_LANES), 1), 1.0, 0.0)
    k = jnp.concatenate([k_ref[0, 0].astype(F32), onehot], axis=1).astype(BF16)
    ones_col = jnp.where(lax.broadcasted_iota(jnp.int32, (t, NSA_HD), 1) == 0, 1.0, 0.0)
    v = jnp.concatenate([v_ref[0, 0].astype(F32), ones_col], axis=1).astype(BF16)

    def sweep(mask_add):
        for hh in range(HPG):
            s = lax.dot_general(q_ref[0, hh], k, (((1,), (1,)), ((), ())), preferred_element_type=F32)
            bias = sl_ref[g * HPG + hh] * rel
            s = s + (bias if mask_add is None else mask_add + bias)
            m_new, _, a_new = _online_update(s, v, m_sc[hh], None, a_sc[hh])
            m_sc[hh] = m_new
            a_sc[hh] = a_new

    @pl.when(kt < qt)
    def _():
        sweep(None)

    @pl.when(kt == qt)
    def _():
        row = lax.broadcasted_iota(jnp.int32, (t, t), 0)
        colf = lax.broadcasted_iota(jnp.int32, (t, t), 1)
        sweep(jnp.where(colf <= row, 0.0, NEG))
        for hh in range(HPG):
            acc = a_sc[hh]
            o_ref[0, hh] = acc[:, :NSA_HD] / acc[:, NSA_HD:NSA_HD + 1]


def _slc_attention(qs, kv4, *, t=1024):
    b, _, s, w = qs.shape
    t = min(t, s)
    qi, ki = _tri_steps(s // t)
    slopes = jnp.asarray(_alibi_slopes_log2(NSA_HEADS))
    grid_spec = pltpu.PrefetchScalarGridSpec(
        num_scalar_prefetch=3,
        grid=(b, NSA_KV, len(qi)),
        in_specs=[pl.BlockSpec((1, HPG, t, w), lambda i, g, st, qi, ki, sl: (i, g, qi[st], 0)),
                  pl.BlockSpec((1, 1, t, NSA_HD), lambda i, g, st, qi, ki, sl: (2 * NSA_KV + g, i, ki[st], 0)),
                  pl.BlockSpec((1, 1, t, NSA_HD), lambda i, g, st, qi, ki, sl: (3 * NSA_KV + g, i, ki[st], 0))],
        out_specs=pl.BlockSpec((1, HPG, t, NSA_HD), lambda i, g, st, qi, ki, sl: (i, g, qi[st], 0)),
        scratch_shapes=[pltpu.VMEM((HPG, t, LANES), F32), pltpu.VMEM((HPG, t, 2 * NSA_HD), F32)],
    )
    return pl.pallas_call(
        functools.partial(_slc_kernel, t=t),
        grid_spec=grid_spec,
        out_shape=jax.ShapeDtypeStruct((b, NSA_HEADS, s, NSA_HD), F32),
        compiler_params=_cparams(("parallel", "parallel", "arbitrary")),
        name="nsa_slc_attention",
    )(jnp.asarray(qi), jnp.asarray(ki), slopes, qs, kv4, kv4)


WIN_ROWS = 256


def _win_kernel(sl_ref, q_ref, kp_ref, kd_ref, vp_ref, vd_ref, oc_ref, os_ref, g_ref, o_ref, *, t):
    g = pl.program_id(1)
    qt = pl.program_id(2)
    kband = jnp.concatenate([kp_ref[0, 0], kd_ref[0, 0]], axis=0)
    vband = jnp.concatenate([vp_ref[0, 0], vd_ref[0, 0]], axis=0)
    span = WIN + WIN_ROWS
    ii = lax.broadcasted_iota(jnp.int32, (WIN_ROWS, span), 0)
    jj = lax.broadcasted_iota(jnp.int32, (WIN_ROWS, span), 1)
    jr = lax.broadcasted_iota(jnp.int32, (1, span), 1)
    in_window = (jj > ii) & (jj <= ii + WIN)
    outs = [[] for _ in range(HPG)]
    for r in range(t // WIN_ROWS):
        r0 = r * WIN_ROWS
        exists = (jj + r0 >= WIN) | (qt > 0)
        mask_add = jnp.where(in_window & exists, 0.0, NEG)
        rel = (jr + (r0 - WIN)).astype(F32)
        ks = kband[r0:r0 + span]
        vs = vband[r0:r0 + span]
        for hh in range(HPG):
            q = q_ref[0, hh][r0:r0 + WIN_ROWS, :NSA_HD]
            s = lax.dot_general(q, ks, (((1,), (1,)), ((), ())), preferred_element_type=F32)
            s = s + (mask_add + sl_ref[g * HPG + hh] * rel)
            _, l, acc = _online_update(s, vs, jnp.full((WIN_ROWS, LANES), NEG, F32),
                                       jnp.zeros((WIN_ROWS, LANES), F32), jnp.zeros((WIN_ROWS, NSA_HD), F32))
            outs[hh].append(acc / _row_total(l))
    res = []
    for hh in range(HPG):
        o_win = jnp.concatenate(outs[hh], axis=0)
        gt = g_ref[0, hh]
        res.append(gt[:, 0:1] * oc_ref[0, hh] + gt[:, 1:2] * os_ref[0, hh] + gt[:, 2:3] * o_win)
    o_ref[0] = jnp.concatenate(res, axis=1).astype(o_ref.dtype)


def _win_combine(qs, kv4, o_cmp, o_slc, gates, *, t=512):
    b, _, s, w = qs.shape
    t = min(t, s)
    assert t == WIN and t % WIN_ROWS == 0
    nblk = 1
    slopes = jnp.asarray(_alibi_slopes_log2(NSA_HEADS))
    kv = lambda kind, off: pl.BlockSpec(
        (1, 1, t, NSA_HD), lambda i, g, qt, sl: (kind * NSA_KV + g, i, jnp.maximum(qt + off, 0), 0))
    per_head = lambda c: pl.BlockSpec((1, HPG, t, c), lambda i, g, qt, sl: (i, g, qt, 0))
    grid_spec = pltpu.PrefetchScalarGridSpec(
        num_scalar_prefetch=1,
        grid=(b, NSA_KV, s // t),
        in_specs=([per_head(w)] + [kv(4, off) for off in range(-nblk, 1)]
                  + [kv(5, off) for off in range(-nblk, 1)]
                  + [per_head(NSA_HD), per_head(NSA_HD), per_head(3)]),
        out_specs=pl.BlockSpec((1, t, HPG * NSA_HD), lambda i, g, qt, sl: (i, qt, g)),
    )
    return pl.pallas_call(
        functools.partial(_win_kernel, t=t),
        grid_spec=grid_spec,
        out_shape=jax.ShapeDtypeStruct((b, s, NSA_Q), BF16),
        compiler_params=_cparams(("parallel", "parallel", "parallel")),
        name="nsa_win_combine",
    )(slopes, qs, *([kv4] * (2 * nblk + 2)), o_cmp, o_slc, gates)


def _mix_kernel(x_ref, od_ref, on_ref, wg1_ref, wg2_ref, bg1_ref, bg2_ref, wd_ref, wn_ref, o_ref, xb_ref):
    @pl.when(pl.program_id(1) == 0)
    def _():
        xb_ref[...] = x_ref[...].astype(BF16)

    xb = xb_ref[...]
    g1 = jax.nn.sigmoid(jnp.dot(xb, wg1_ref[...], preferred_element_type=F32) + bg1_ref[...])
    g2 = jax.nn.sigmoid(jnp.dot(xb, wg2_ref[...], preferred_element_type=F32) + bg2_ref[...])
    bd = jnp.dot(od_ref[...], wd_ref[...], preferred_element_type=F32)
    bn = jnp.dot(on_ref[...], wn_ref[...], preferred_element_type=F32)
    o_ref[...] = (g1 * bd + g2 * bn).astype(o_ref.dtype)


def _mix(x2, o_diff, o_nsa, w_mgate, b_mgate, w_br_diff, w_br_nsa, *, tm=512, tn=512):
    m, d = x2.shape
    tm = min(tm, m)
    nj = d // tn
    wg = w_mgate.astype(BF16)
    bg = b_mgate.reshape(1, -1).astype(F32)
    row = pl.BlockSpec((tm, d), lambda i, j: (i, 0))
    return pl.pallas_call(
        _mix_kernel,
        grid=(m // tm, nj),
        in_specs=[row, row, row,
                  pl.BlockSpec((d, tn), lambda i, j: (0, j)),
                  pl.BlockSpec((d, tn), lambda i, j: (0, j + nj)),
                  pl.BlockSpec((1, tn), lambda i, j: (0, j)),
                  pl.BlockSpec((1, tn), lambda i, j: (0, j + nj)),
                  pl.BlockSpec((d, tn), lambda i, j: (0, j)),
                  pl.BlockSpec((d, tn), lambda i, j: (0, j))],
        out_specs=pl.BlockSpec((tm, tn), lambda i, j: (i, j)),
        out_shape=jax.ShapeDtypeStruct((m, d), BF16),
        scratch_shapes=[pltpu.VMEM((tm, d), BF16)],
        compiler_params=_cparams(("parallel", "arbitrary")),
        name="branch_mix",
    )(x2, o_diff, o_nsa, wg, wg, bg, bg, w_br_diff.astype(BF16), w_br_nsa.astype(BF16))


def _layer_norm_rows(y, g, b):
    mu = jnp.mean(y, axis=-1, keepdims=True)
    yc = y - mu
    var = jnp.mean(yc * yc, axis=-1, keepdims=True)
    return yc * lax.rsqrt(var + LN_EPS) * g + b


def _split_bf16(a):
    hi = a.astype(BF16)
    lo = (a - hi.astype(F32)).astype(BF16)
    return hi, lo


def _post_attn_kernel(mx_ref, x_ref, wo_ref, g_ref, b_ref, wr_hi_ref, wr_lo_ref, br_ref,
                      h_ref, e_ref, gate_ref, rank_ref, cnt_ref, run_ref, *, tm, alpha):
    i = pl.program_id(0)

    @pl.when(i == 0)
    def _():
        run_ref[...] = jnp.zeros(run_ref.shape, F32)

    y = alpha * x_ref[...] + jnp.dot(mx_ref[...], wo_ref[...], preferred_element_type=F32)
    h = _layer_norm_rows(y, g_ref[...], b_ref[...])
    h_ref[...] = h
    h_hi, h_lo = _split_bf16(h)
    logits = (jnp.dot(h_hi, wr_hi_ref[...], preferred_element_type=F32)
              + jnp.dot(h_hi, wr_lo_ref[...], preferred_element_type=F32)
              + jnp.dot(h_lo, wr_hi_ref[...], preferred_element_type=F32)) + br_ref[...]
    lane = lax.broadcasted_iota(jnp.int32, (tm, N_EXPERTS), 1)
    lane4 = lax.broadcasted_iota(jnp.int32, (tm, TOP_K), 1)
    xl = logits
    vals, idxs, hits = [], [], []
    for _ in range(TOP_K):
        mx = jnp.max(xl, axis=-1, keepdims=True)
        idx = jnp.min(jnp.where(xl == mx, lane, N_EXPERTS), axis=-1, keepdims=True)
        hit = lane == idx
        vals.append(mx)
        idxs.append(idx)
        hits.append(hit)
        xl = jnp.where(hit, LOWEST, xl)
    ex = [jnp.exp(v - vals[0]) for v in vals]
    den = ex[0] + ex[1] + ex[2] + ex[3]
    member = jnp.zeros((tm, N_EXPERTS), F32)
    for hit in hits:
        member = jnp.where(hit, 1.0, member)
    r_i = lax.broadcasted_iota(jnp.int32, (tm, tm), 0)
    c_i = lax.broadcasted_iota(jnp.int32, (tm, tm), 1)
    tri = jnp.where(c_i < r_i, 1.0, 0.0).astype(BF16)
    before = jnp.dot(tri, member.astype(BF16), preferred_element_type=F32) + run_ref[...]
    e_out = jnp.zeros((tm, TOP_K), jnp.int32)
    g_out = jnp.zeros((tm, TOP_K), F32)
    r_out = jnp.zeros((tm, TOP_K), jnp.int32)
    for k in range(TOP_K):
        rk = jnp.sum(jnp.where(hits[k], before, 0.0), axis=-1, keepdims=True)
        e_out = jnp.where(lane4 == k, idxs[k], e_out)
        g_out = jnp.where(lane4 == k, ex[k] / den, g_out)
        r_out = jnp.where(lane4 == k, rk.astype(jnp.int32), r_out)
    e_ref[...] = e_out
    gate_ref[...] = g_out
    rank_ref[...] = r_out
    run_ref[...] = run_ref[...] + jnp.sum(member, axis=0, keepdims=True)
    cnt_ref[...] = run_ref[...].astype(jnp.int32)


def _post_attn(mixed, x2, w_o, ln_g, ln_b, w_router, b_router, *, alpha, tm=256):
    m, d = x2.shape
    tm = min(tm, m)
    wr_hi, wr_lo = _split_bf16(w_router.astype(F32))
    vec = lambda a: a.reshape(1, -1).astype(F32)
    full = lambda r, c: pl.BlockSpec((r, c), lambda i: (0, 0))
    rows = lambda c: pl.BlockSpec((tm, c), lambda i: (i, 0))
    return pl.pallas_call(
        functools.partial(_post_attn_kernel, tm=tm, alpha=alpha),
        grid=(m // tm,),
        in_specs=[rows(d), rows(d), full(d, d), full(1, d), full(1, d),
                  full(d, N_EXPERTS), full(d, N_EXPERTS), full(1, N_EXPERTS)],
        out_specs=[rows(d), rows(TOP_K), rows(TOP_K), rows(TOP_K), full(1, N_EXPERTS)],
        out_shape=[jax.ShapeDtypeStruct((m, d), F32),
                   jax.ShapeDtypeStruct((m, TOP_K), jnp.int32),
                   jax.ShapeDtypeStruct((m, TOP_K), F32),
                   jax.ShapeDtypeStruct((m, TOP_K), jnp.int32),
                   jax.ShapeDtypeStruct((1, N_EXPERTS), jnp.int32)],
        scratch_shapes=[pltpu.VMEM((1, N_EXPERTS), F32)],
        compiler_params=_cparams(("arbitrary",)),
        name="post_attn_router",
    )(mixed, x2, w_o.astype(BF16), vec(ln_g), vec(ln_b), wr_hi, wr_lo, vec(b_router))


def _dest_kernel(ps_ref, e_ref, r_ref, o_ref):
    e = e_ref[...]
    base = jnp.zeros(e.shape, jnp.int32)
    for j in range(N_EXPERTS):
        base = jnp.where(e == j, ps_ref[j], base)
    o_ref[...] = base + r_ref[...]


def _dest(top_e, rank, pstart, *, tm=2048):
    m = top_e.shape[0]
    tm = min(tm, m)
    grid_spec = pltpu.PrefetchScalarGridSpec(
        num_scalar_prefetch=1,
        grid=(m // tm,),
        in_specs=[pl.BlockSpec((tm, TOP_K), lambda i, ps: (i, 0)),
                  pl.BlockSpec((tm, TOP_K), lambda i, ps: (i, 0))],
        out_specs=pl.BlockSpec((tm, TOP_K), lambda i, ps: (i, 0)),
    )
    return pl.pallas_call(
        _dest_kernel, grid_spec=grid_spec,
        out_shape=jax.ShapeDtypeStruct((m, TOP_K), jnp.int32),
        compiler_params=_cparams(("parallel",)),
        name="moe_dest",
    )(pstart, top_e, rank)


N_DMA_PRIORITIES = 2


def _dispatch_kernel(zf_ref, dest_ref, h_ref, xs_ref, zeros, sem, zsem, *, tm, n_blocks):
    @pl.when(pl.program_id(0) == 0)
    def _():
        zeros[...] = jnp.zeros(zeros.shape, zeros.dtype)

        def zero_copy(blk):
            start = pl.multiple_of(blk * MOE_BLOCK, MOE_BLOCK)
            return pltpu.make_async_copy(zeros, xs_ref.at[pl.ds(start, MOE_BLOCK), :], zsem)

        def zstart(blk, carry):
            @pl.when(zf_ref[blk] > 0)
            def _():
                zero_copy(blk).start()
            return carry

        def zwait(blk, carry):
            @pl.when(zf_ref[blk] > 0)
            def _():
                zero_copy(blk).wait()
            return carry

        lax.fori_loop(0, n_blocks, zstart, 0)
        lax.fori_loop(0, n_blocks, zwait, 0)

    def row_copy(r, k):
        return pltpu.make_async_copy(h_ref.at[pl.ds(r, 1), :],
                                     xs_ref.at[pl.ds(dest_ref[r * TOP_K + k], 1), :], sem)

    for r in range(tm):
        for k in range(TOP_K):
            row_copy(r, k).start(priority=k % N_DMA_PRIORITIES)
    for r in range(tm):
        for k in range(TOP_K):
            row_copy(r, k).wait()


def _dispatch(h, dest_flat, zero_flag, n_pad, *, tm=256):
    m, d = h.shape
    tm = min(tm, m)
    assert m % tm == 0
    n_blocks = n_pad // MOE_BLOCK
    grid_spec = pltpu.PrefetchScalarGridSpec(
        num_scalar_prefetch=1,
        grid=(m // tm,),
        in_specs=[pl.BlockSpec((tm * TOP_K,), lambda i, zf: (i,), memory_space=pltpu.SMEM),
                  pl.BlockSpec((tm, d), lambda i, zf: (i, 0))],
        out_specs=pl.BlockSpec(memory_space=pl.ANY),
        scratch_shapes=[pltpu.VMEM((MOE_BLOCK, d), h.dtype), pltpu.SemaphoreType.DMA(()),
                        pltpu.SemaphoreType.DMA(())],
    )
    return pl.pallas_call(
        functools.partial(_dispatch_kernel, tm=tm, n_blocks=n_blocks),
        grid_spec=grid_spec,
        out_shape=jax.ShapeDtypeStruct((n_pad, d), h.dtype),
        compiler_params=_cparams(("arbitrary",)),
        name="moe_dispatch",
    )(zero_flag, dest_flat, h)


def _expert_kernel(be_ref, nu_ref, xs_ref, w1_ref, b1_ref, w2_ref, b2_ref, y_ref, w1b, w2b):
    i = pl.program_id(0)
    new_expert = (i == 0) | (be_ref[i] != be_ref[jnp.maximum(i - 1, 0)])

    @pl.when((i < nu_ref[0]) & new_expert)
    def _():
        w1b[...] = w1_ref[0].astype(BF16)
        w2b[...] = w2_ref[0].astype(BF16)

    @pl.when(i < nu_ref[0])
    def _():
        xb = xs_ref[...].astype(BF16)
        hb = jnp.dot(xb, w1b[...], preferred_element_type=F32) + b1_ref[0]
        glu = jnp.minimum(hb[:, :D_FF], SWIGLU_LIMIT)
        lin = jnp.clip(hb[:, D_FF:], -SWIGLU_LIMIT, SWIGLU_LIMIT)
        act = glu * jax.nn.sigmoid(SWIGLU_ALPHA * glu) * (lin + 1.0)
        y_ref[...] = jnp.dot(act.astype(BF16), w2b[...], preferred_element_type=F32) + b2_ref[0]

    @pl.when(i >= nu_ref[0])
    def _():
        y_ref[...] = jnp.zeros(y_ref.shape, y_ref.dtype)


def _experts(xs, blk_expert, n_used, w1, b1, w2, b2):
    n_pad, d = xs.shape
    n_blocks = n_pad // MOE_BLOCK
    last = lambda i, nu: jnp.minimum(i, nu[0] - 1)
    grid_spec = pltpu.PrefetchScalarGridSpec(
        num_scalar_prefetch=2,
        grid=(n_blocks,),
        in_specs=[pl.BlockSpec((MOE_BLOCK, d), lambda i, be, nu: (last(i, nu), 0)),
                  pl.BlockSpec((1, d, 2 * D_FF), lambda i, be, nu: (be[last(i, nu)], 0, 0)),
                  pl.BlockSpec((1, 1, 2 * D_FF), lambda i, be, nu: (be[last(i, nu)], 0, 0)),
                  pl.BlockSpec((1, D_FF, d), lambda i, be, nu: (be[last(i, nu)], 0, 0)),
                  pl.BlockSpec((1, 1, d), lambda i, be, nu: (be[last(i, nu)], 0, 0))],
        out_specs=pl.BlockSpec((MOE_BLOCK, d), lambda i, be, nu: (i, 0)),
        scratch_shapes=[pltpu.VMEM((d, 2 * D_FF), BF16), pltpu.VMEM((D_FF, d), BF16)],
    )
    return pl.pallas_call(
        _expert_kernel, grid_spec=grid_spec,
        out_shape=jax.ShapeDtypeStruct((n_pad, d), F32),
        compiler_params=_cparams(("arbitrary",), vmem=EXPERT_VMEM_LIMIT),
        name="moe_experts",
    )(blk_expert, n_used, xs, w1.astype(F32), b1.reshape(N_EXPERTS, 1, -1).astype(F32),
      w2.astype(F32), b2.reshape(N_EXPERTS, 1, -1).astype(F32))


def _final_kernel(dest_ref, h_ref, gate_ref, p_ref, wpg_ref, bpg_ref, wpp_ref, g_ref, b_ref, y_ref,
                  o_ref, buf, sem, *, tm, alpha):
    def row_copy(r, k):
        return pltpu.make_async_copy(y_ref.at[pl.ds(dest_ref[r * TOP_K + k], 1), :],
                                     buf.at[k, pl.ds(r, 1), :], sem)

    for r in range(tm):
        for k in range(TOP_K):
            row_copy(r, k).start(priority=k % N_DMA_PRIORITIES)
    h = h_ref[...]
    ple = (jax.nn.sigmoid(jnp.dot(h.astype(BF16), wpg_ref[...], preferred_element_type=F32) + bpg_ref[...])
           * jnp.dot(p_ref[...].astype(BF16), wpp_ref[...], preferred_element_type=F32))
    for r in range(tm):
        for k in range(TOP_K):
            row_copy(r, k).wait()
    gate = gate_ref[...]
    ffn = gate[:, 0:1] * buf[0]
    for k in range(1, TOP_K):
        ffn = ffn + gate[:, k:k + 1] * buf[k]
    o_ref[...] = _layer_norm_rows(alpha * h + ffn + ple, g_ref[...], b_ref[...])


def _final(h, gate, dest_flat, y, p2, w_ple_gate, b_ple_gate, w_ple_proj, ln_g, ln_b, *, alpha, tm=256):
    m, d = h.shape
    tm = min(tm, m)
    vec = lambda a: a.reshape(1, -1).astype(F32)
    full = lambda r, c: pl.BlockSpec((r, c), lambda i: (0, 0))
    rows = lambda c: pl.BlockSpec((tm, c), lambda i: (i, 0))
    return pl.pallas_call(
        functools.partial(_final_kernel, tm=tm, alpha=alpha),
        grid=(m // tm,),
        in_specs=[pl.BlockSpec((tm * TOP_K,), lambda i: (i,), memory_space=pltpu.SMEM),
                  rows(d), rows(TOP_K), rows(PLE_DIM), full(d, d), full(1, d), full(PLE_DIM, d),
                  full(1, d), full(1, d), pl.BlockSpec(memory_space=pl.ANY)],
        out_specs=rows(d),
        out_shape=jax.ShapeDtypeStruct((m, d), F32),
        scratch_shapes=[pltpu.VMEM((TOP_K, tm, d), F32), pltpu.SemaphoreType.DMA(())],
        compiler_params=_cparams(("arbitrary",)),
        name="moe_combine_final",
    )(dest_flat, h, gate, p2, w_ple_gate.astype(BF16), vec(b_ple_gate), w_ple_proj.astype(BF16),
      vec(ln_g), vec(ln_b), y)


def _attention_block(x2, b, s, w_in, diff_p, nsa_p, lambda_init):
    t = b * s
    qkv, kv, gates = _input_projection(x2, w_in)
    qkv3 = qkv.reshape(b, s, C_QKV)
    kv4 = kv.reshape(N_KV, b, s, NSA_HD)
    o_diff = _diff_attention(qkv3, *diff_p, lambda_init)

    halves = lambda kind: kv[kind * NSA_KV:(kind + 1) * NSA_KV].reshape(
        NSA_KV, b, s // CMP_STRIDE, CMP_STRIDE * NSA_HD)
    pos_k, pos_v, phi_k1, phi_k2, phi_v1, phi_v2 = nsa_p
    k_cmp = _compress(halves(0), pos_k, phi_k1, phi_k2)
    v_cmp = _compress(halves(1), pos_v, phi_v1, phi_v2)
    o_cmp, qs = _cmp_select(qkv3, k_cmp, v_cmp)
    o_slc = _slc_attention(qs, kv4)
    g4 = gates[:, :NSA_GATES].reshape(b, s, NSA_HEADS, 3).transpose(0, 2, 1, 3)
    o_nsa = _win_combine(qs, kv4, o_cmp, o_slc, g4)
    return o_diff.reshape(t, DIFF_V), o_nsa.reshape(t, NSA_Q)


def _moe_block(h, top_e, gate, rank, counts, w1, b1, w2, b2):
    t = h.shape[0]
    n_blocks = (t * TOP_K + MOE_BLOCK - 1) // MOE_BLOCK + N_EXPERTS
    n_pad = n_blocks * MOE_BLOCK
    counts = counts.reshape(N_EXPERTS)
    pcounts = (counts + MOE_BLOCK - 1) // MOE_BLOCK * MOE_BLOCK
    pend = jnp.cumsum(pcounts)
    pstart = (pend - pcounts).astype(jnp.int32)
    blk_start = jnp.arange(n_blocks, dtype=jnp.int32) * MOE_BLOCK
    blk_expert = jnp.minimum(jnp.sum(pend[None, :] <= blk_start[:, None], axis=1),
                             N_EXPERTS - 1).astype(jnp.int32)
    n_used = (pend[-1:] // MOE_BLOCK).astype(jnp.int32)
    zero_flag = ((blk_start + MOE_BLOCK == pend[blk_expert])
                 | (blk_start >= pend[-1])).astype(jnp.int32)
    dest = _dest(top_e, rank, pstart).reshape(t * TOP_K)
    xs = _dispatch(h, dest, zero_flag, n_pad)
    y = _experts(xs, blk_expert, n_used, w1, b1, w2, b2)
    return dest, y


def kernel(x, p, w_in, diff_lq1, diff_lk1, diff_lq2, diff_lk2, diff_subln_g, nsa_pos_k, nsa_pos_v,
           nsa_phi_k1, nsa_phi_k2, nsa_phi_v1, nsa_phi_v2, w_br_diff, w_br_nsa, w_mgate, b_mgate, w_o,
           ln1_g, ln1_b, w_router, b_router, w_e1, b_e1, w_e2, b_e2, w_ple_gate, b_ple_gate,
           w_ple_proj, ln2_g, ln2_b):
    b, s, d = x.shape
    depth = w_in.shape[0]
    alpha = (2.0 * depth) ** 0.25
    x2 = x.reshape(b * s, d)
    for i in range(depth):
        lambda_init = 0.8 - 0.6 * math.exp(-0.3 * i)
        o_diff, o_nsa = _attention_block(
            x2, b, s, w_in[i],
            (diff_lq1[i], diff_lk1[i], diff_lq2[i], diff_lk2[i], diff_subln_g[i]),
            (nsa_pos_k[i], nsa_pos_v[i], nsa_phi_k1[i], nsa_phi_k2[i], nsa_phi_v1[i], nsa_phi_v2[i]),
            lambda_init)
        mixed = _mix(x2, o_diff, o_nsa, w_mgate[i], b_mgate[i], w_br_diff[i], w_br_nsa[i])
        h, top_e, gate, rank, counts = _post_attn(mixed, x2, w_o[i], ln1_g[i], ln1_b[i],
                                                  w_router[i], b_router[i], alpha=alpha)
        dest, y = _moe_block(h, top_e, gate, rank, counts, w_e1[i], b_e1[i], w_e2[i], b_e2[i])
        x2 = _final(h, gate, dest, y, p[i].reshape(b * s, -1), w_ple_gate[i], b_ple_gate[i],
                    w_ple_proj[i], ln2_g[i], ln2_b[i], alpha=alpha)
    return x2.reshape(b, s, d)
```
